```python
import math
import jax, jax.numpy as jnp
from jax import lax
import numpy as np


D_MODEL = 1024
BATCH = 4
SEQ = 4096
DEPTH = 4
DEC_BATCH = 16
DEC_SEQ = 2048
PAST_LEN = 128

D_ATTN = 512
N_Q_HEADS = 8
N_KV_HEADS = 2
HEAD_DIM = 64
Q_PER_KV = N_Q_HEADS // N_KV_HEADS
KV_DIM = N_KV_HEADS * HEAD_DIM
WINDOW = 128
BLOCK = 128
D_LRU = 512
N_LRU_BLOCKS = 8
LRU_BLOCK = D_LRU // N_LRU_BLOCKS
CONV_WIDTH = 4
CONV_PAD_LEFT = 2
CONV_PAD_RIGHT = CONV_WIDTH - 1 - CONV_PAD_LEFT
LRU_C = 8.0
N_DIR = 2
D_MIX = D_ATTN + D_LRU
D_IN = D_ATTN + 2 * KV_DIM + 2 * D_LRU
D_FF = ((8 * D_MODEL + 2) // 3 + 255) // 256 * 256
N_MOD = 6
EPS = 1e-6

kernel_name = 'hybrid_bidir_rglru_swa_encoder'


def rmsnorm(x, g):
    xf = x.astype(jnp.float32)
    y = xf * lax.rsqrt(jnp.mean(xf * xf, axis=-1, keepdims=True) + EPS)
    return (y * g.astype(jnp.float32)).astype(x.dtype)


def alibi_slopes():
    h = jnp.arange(N_Q_HEADS, dtype=jnp.float32) + 1.0
    return jnp.exp2(-8.0 * h / N_Q_HEADS)


def banded_attention(q, k, v, sink):
    B, S = q.shape[0], q.shape[1]
    nb = S // BLOCK
    f32 = jnp.float32
    qb = q.astype(f32).reshape(B, nb, BLOCK, N_KV_HEADS, Q_PER_KV, HEAD_DIM)
    pad = ((0, 0), (BLOCK, BLOCK), (0, 0))
    kp = jnp.pad(k.astype(f32), pad).reshape(B, nb + 2, BLOCK, N_KV_HEADS, HEAD_DIM)
    vp = jnp.pad(v.astype(f32), pad).reshape(B, nb + 2, BLOCK, N_KV_HEADS, HEAD_DIM)
    kb = jnp.concatenate([kp[:, :-2], kp[:, 1:-1], kp[:, 2:]], axis=2)
    vb = jnp.concatenate([vp[:, :-2], vp[:, 1:-1], vp[:, 2:]], axis=2)
    scores = jnp.einsum('bnqkgd,bnskd->bnkgqs', qb, kb) * (HEAD_DIM ** -0.5)
    tq = jnp.arange(S).reshape(nb, BLOCK)
    ts = (jnp.arange(nb)[:, None] - 1) * BLOCK + jnp.arange(3 * BLOCK)[None, :]
    dist = jnp.abs(tq[:, :, None] - ts[:, None, :])
    valid = (dist <= WINDOW) & (ts[:, None, :] >= 0) & (ts[:, None, :] < S)
    slopes = alibi_slopes().reshape(N_KV_HEADS, Q_PER_KV)
    bias = -slopes[None, :, :, None, None] * dist[:, None, None].astype(f32)
    logits = jnp.where(valid[None, :, None, None], scores + bias[None], -1e30)
    sink_col = jnp.broadcast_to(
        sink.astype(f32).reshape(N_KV_HEADS, Q_PER_KV)[None, None, :, :, None, None],
        logits.shape[:-1] + (1,))
    p = jax.nn.softmax(jnp.concatenate([logits, sink_col], axis=-1), axis=-1)[..., :-1]
    out = jnp.einsum('bnkgqs,bnskd->bnqkgd', p, vb)
    return out.reshape(B, S, D_ATTN).astype(q.dtype)


def _lin_combine(e1, e2):
    a1, b1 = e1
    a2, b2 = e2
    return a1 * a2, a2 * b1 + b2


def rglru_bidir(x, conv_w, conv_b, w_rg, b_rg, w_ig, b_ig, lam):
    B, S = x.shape[0], x.shape[1]
    f32 = jnp.float32
    xp = jnp.pad(x, ((0, 0), (CONV_PAD_LEFT, CONV_PAD_RIGHT), (0, 0)))
    xc = conv_b
    for j in range(CONV_WIDTH):
        xc = xc + xp[:, j:j + S] * conv_w[j]
    xf = xc.astype(f32)
    xblk = xf.reshape(B, S, N_LRU_BLOCKS, LRU_BLOCK)
    r = jax.nn.sigmoid(jnp.einsum('bsnc,dncm->dbsnm', xblk, w_rg.astype(f32)).reshape(N_DIR, B, S, D_LRU)
                       + b_rg.astype(f32)[:, None, None, :])
    i = jax.nn.sigmoid(jnp.einsum('bsnc,dncm->dbsnm', xblk, w_ig.astype(f32)).reshape(N_DIR, B, S, D_LRU)
                       + b_ig.astype(f32)[:, None, None, :])
    log_a = LRU_C * r * jax.nn.log_sigmoid(lam.astype(f32))[:, None, None, :]
    a = jnp.exp(log_a)
    u = jnp.sqrt(jnp.maximum(1.0 - a * a, 0.0)) * (i * xf[None])
    _, h_fwd = lax.associative_scan(_lin_combine, (a[0], u[0]), axis=1)
    _, h_bwd = lax.associative_scan(_lin_combine, (a[1], u[1]), axis=1, reverse=True)
    return (h_fwd + h_bwd).astype(x.dtype)


def trunk(x, c, w_mod, b_mod, g_norm1, w_in, sink, conv_w, conv_b, w_rg, b_rg, w_ig, b_ig, lam,
          g_attn_out, g_lru_out, w_out, g_norm2, w_ffn_in, w_ffn_out, g_final):
    c_act = jax.nn.silu(c)
    s1 = D_ATTN
    s2 = s1 + KV_DIM
    s3 = s2 + KV_DIM
    s4 = s3 + D_LRU
    for l in range(DEPTH):
        mod = (c_act @ w_mod[l] + b_mod[l])[:, None, :]
        sh1, sc1, gt1, sh2, sc2, gt2 = jnp.split(mod, N_MOD, axis=-1)
        h = rmsnorm(x, g_norm1[l]) * (1.0 + sc1) + sh1
        z = h @ w_in[l]
        q, k, v, xr, gate = jnp.split(z, [s1, s2, s3, s4], axis=-1)
        attn = banded_attention(q, k, v, sink[l])
        lru = rglru_bidir(xr, conv_w[l], conv_b[l], w_rg[l], b_rg[l], w_ig[l], b_ig[l], lam[l])
        lru = lru * jax.nn.gelu(gate)
        mix = jnp.concatenate([rmsnorm(attn, g_attn_out[l]), rmsnorm(lru, g_lru_out[l])], axis=-1)
        x = x + gt1 * (mix @ w_out[l])
        h = rmsnorm(x, g_norm2[l]) * (1.0 + sc2) + sh2
        gu = h @ w_ffn_in[l]
        g_, u_ = jnp.split(gu, 2, axis=-1)
        x = x + gt2 * ((jax.nn.silu(g_) * u_) @ w_ffn_out[l])
    return rmsnorm(x, g_final)


def setup_inputs(seed: int = 0) -> dict:
    key = jax.random.key(seed)
    ks = jax.random.split(key, 32)
    f32 = jnp.float32
    nrm = lambda k, shape, s: jax.random.normal(k, shape, f32) * s
    u = jax.random.uniform(ks[14], (DEPTH, N_DIR, D_LRU), f32, 0.9, 0.999)
    p = u ** (1.0 / LRU_C)
    lam = jnp.log(p) - jnp.log1p(-p)
    return {
        'x_prompt': nrm(ks[0], (BATCH, SEQ, D_MODEL), 1.0),
        'x_sample': nrm(ks[1], (DEC_BATCH, DEC_SEQ, D_MODEL), 1.0),
        'c_prompt': nrm(ks[2], (BATCH, D_MODEL), 1.0),
        'c_sample': nrm(ks[3], (DEC_BATCH, D_MODEL), 1.0),
        'w_mod': nrm(ks[4], (DEPTH, D_MODEL, N_MOD * D_MODEL), 0.5 * D_MODEL ** -0.5),
        'b_mod': nrm(ks[5], (DEPTH, N_MOD * D_MODEL), 0.02),
        'g_norm1': 1.0 + nrm(ks[6], (DEPTH, D_MODEL), 0.02),
        'w_in': nrm(ks[7], (DEPTH, D_MODEL, D_IN), D_MODEL ** -0.5),
        'sink': nrm(ks[8], (DEPTH, N_Q_HEADS), 0.5),
        'conv_w': nrm(ks[9], (DEPTH, CONV_WIDTH, D_LRU), CONV_WIDTH ** -0.5),
        'conv_b': nrm(ks[10], (DEPTH, D_LRU), 0.02),
        'w_rg': nrm(ks[11], (DEPTH, N_DIR, N_LRU_BLOCKS, LRU_BLOCK, LRU_BLOCK), LRU_BLOCK ** -0.5),
        'b_rg': nrm(ks[12], (DEPTH, N_DIR, D_LRU), 0.02),
        'w_ig': nrm(ks[13], (DEPTH, N_DIR, N_LRU_BLOCKS, LRU_BLOCK, LRU_BLOCK), LRU_BLOCK ** -0.5),
        'b_ig': nrm(ks[15], (DEPTH, N_DIR, D_LRU), 0.02),
        'lam': lam,
        'g_attn_out': 1.0 + nrm(ks[16], (DEPTH, D_ATTN), 0.02),
        'g_lru_out': 1.0 + nrm(ks[17], (DEPTH, D_LRU), 0.02),
        'w_out': nrm(ks[18], (DEPTH, D_MIX, D_MODEL), D_MIX ** -0.5),
        'g_norm2': 1.0 + nrm(ks[19], (DEPTH, D_MODEL), 0.02),
        'w_ffn_in': nrm(ks[20], (DEPTH, D_MODEL, 2 * D_FF), D_MODEL ** -0.5),
        'w_ffn_out': nrm(ks[21], (DEPTH, D_FF, D_MODEL), D_FF ** -0.5),
        'g_final': 1.0 + nrm(ks[22], (D_MODEL,), 0.02),
    }


def reference(x_prompt, x_sample, c_prompt, c_sample, w_mod, b_mod, g_norm1, w_in, sink, conv_w, conv_b,
              w_rg, b_rg, w_ig, b_ig, lam, g_attn_out, g_lru_out, w_out, g_norm2, w_ffn_in, w_ffn_out, g_final):
    y_prompt = trunk(x_prompt, c_prompt, w_mod, b_mod, g_norm1, w_in, sink, conv_w, conv_b, w_rg, b_rg,
                     w_ig, b_ig, lam, g_attn_out, g_lru_out, w_out, g_norm2, w_ffn_in, w_ffn_out, g_final)
    y_sample = trunk(x_sample, c_sample, w_mod, b_mod, g_norm1, w_in, sink, conv_w, conv_b, w_rg, b_rg,
                     w_ig, b_ig, lam, g_attn_out, g_lru_out, w_out, g_norm2, w_ffn_in, w_ffn_out, g_final)
    return (y_prompt, y_sample)
```

```python
import functools

import jax
import jax.numpy as jnp
import numpy as np
from jax import lax
from jax.experimental import pallas as pl
from jax.experimental.pallas import tpu as pltpu

F32 = jnp.float32
BF16 = jnp.bfloat16

D_MODEL = 1024
DEPTH = 4
D_ATTN = 512
N_Q_HEADS = 8
N_KV_HEADS = 2
HEAD_DIM = 64
Q_PER_KV = N_Q_HEADS // N_KV_HEADS
KV_DIM = N_KV_HEADS * HEAD_DIM
WINDOW = 128
BLOCK = 128
D_LRU = 512
N_LRU_BLOCKS = 8
LRU_BLOCK = D_LRU // N_LRU_BLOCKS
CONV_WIDTH = 4
CONV_PAD_LEFT = 2
LRU_C = 8.0
N_DIR = 2
D_IN = D_ATTN + 2 * KV_DIM + 2 * D_LRU
D_FF = 2816
N_MOD = 6
EPS = 1e-6
MASKED_LOGIT = -1e30

LANES = 128
SUBLANES = 8
VMEM_LIMIT_BYTES = 56 * 1024 * 1024

TOKEN_TILE = 512
MOD_COL_TILE = 1536
FF_COL_TILE = 256
LRU_TILES = D_LRU // LANES
LRU_SUPER = 256
LRU_SUB = 64
N_KEY_BLOCKS = 3
N_BIAS_VARIANTS = 3

P_CONV_W = 0
P_CONV_B = 4
P_B_RG = 5
P_B_IG = 6
P_LAM = 9
P_ROWS = 16


def _rms(x):
    return x * lax.rsqrt(jnp.mean(x * x, axis=-1, keepdims=True) + EPS)


def _compiler_params(n_grid_dims):
    return pltpu.CompilerParams(
        dimension_semantics=("arbitrary",) * n_grid_dims,
        vmem_limit_bytes=VMEM_LIMIT_BYTES,
    )


def _mod_kernel(c_ref, w_ref, b_ref, o_ref):
    c = c_ref[...]
    c_act = c * jax.nn.sigmoid(c)
    o_ref[0] = jnp.dot(c_act, w_ref[0], preferred_element_type=F32) + b_ref[0]


def _modulation(c_all, w_mod, b_mod):
    n_rows = c_all.shape[0]
    n_cols = N_MOD * D_MODEL
    return pl.pallas_call(
        _mod_kernel,
        out_shape=jax.ShapeDtypeStruct((DEPTH, n_rows, n_cols), F32),
        grid=(DEPTH, n_cols // MOD_COL_TILE),
        in_specs=[
            pl.BlockSpec((n_rows, D_MODEL), lambda l, j: (0, 0)),
            pl.BlockSpec((1, D_MODEL, MOD_COL_TILE), lambda l, j: (l, 0, j)),
            pl.BlockSpec((1, 1, MOD_COL_TILE), lambda l, j: (l, 0, j)),
        ],
        out_specs=pl.BlockSpec((1, n_rows, MOD_COL_TILE), lambda l, j: (l, 0, j)),
        compiler_params=_compiler_params(2),
        name="modulation",
    )(c_all, w_mod, b_mod.reshape(DEPTH, 1, n_cols))


def _inproj_kernel(x_ref, mod_ref, g_ref, w_ref, q_ref, k_ref, v_ref, xr_ref, gate_ref):
    x = x_ref[0]
    mod = mod_ref[0]
    h = _rms(x) * (g_ref[...] * (1.0 + mod[1:2])) + mod[0:1]
    z = jnp.dot(h.astype(BF16), w_ref[...], preferred_element_type=F32)
    q_ref[0] = (z[:, :D_ATTN] * (HEAD_DIM ** -0.5)).astype(BF16)
    k_ref[0] = z[:, D_ATTN:D_ATTN + KV_DIM].astype(BF16)
    v_ref[0] = z[:, D_ATTN + KV_DIM:D_ATTN + 2 * KV_DIM].astype(BF16)
    base = D_ATTN + 2 * KV_DIM
    for c in range(LRU_TILES):
        xr_ref[0, c] = z[:, base + c * LANES:base + (c + 1) * LANES]
        gate_ref[0, c] = z[:, base + D_LRU + c * LANES:base + D_LRU + (c + 1) * LANES]


def _in_projection(x, mod, g_norm1, w_in):
    batch, seq, _ = x.shape
    tm = TOKEN_TILE
    tok = lambda b, i: (b, i, 0)
    lru_spec = pl.BlockSpec((1, LRU_TILES, tm, LANES), lambda b, i: (b, 0, i, 0))
    return pl.pallas_call(
        _inproj_kernel,
        out_shape=(
            jax.ShapeDtypeStruct((batch, seq, D_ATTN), BF16),
            jax.ShapeDtypeStruct((batch, seq, KV_DIM), BF16),
            jax.ShapeDtypeStruct((batch, seq, KV_DIM), BF16),
            jax.ShapeDtypeStruct((batch, LRU_TILES, seq, LANES), F32),
            jax.ShapeDtypeStruct((batch, LRU_TILES, seq, LANES), F32),
        ),
        grid=(batch, seq // tm),
        in_specs=[
            pl.BlockSpec((1, tm, D_MODEL), tok),
            pl.BlockSpec((1, N_MOD, D_MODEL), lambda b, i: (b, 0, 0)),
            pl.BlockSpec((1, D_MODEL), lambda b, i: (0, 0)),
            pl.BlockSpec((D_MODEL, D_IN), lambda b, i: (0, 0)),
        ],
        out_specs=(
            pl.BlockSpec((1, tm, D_ATTN), tok),
            pl.BlockSpec((1, tm, KV_DIM), tok),
            pl.BlockSpec((1, tm, KV_DIM), tok),
            lru_spec,
            lru_spec,
        ),
        compiler_params=_compiler_params(2),
        name="in_projection",
    )(x, mod, g_norm1, w_in)


def _slab_head(j):
    return j // 2 + Q_PER_KV * (j % 2)


def _attn_kernel(sink_ref, q_ref, k_ref, v_ref, o_ref, bias_ref):
    n = pl.program_id(1)
    n_blocks = pl.num_programs(1)
    n_slabs = N_Q_HEADS
    n_keys = N_KEY_BLOCKS * BLOCK

    @pl.when((pl.program_id(0) == 0) & (n == 0))
    def _init_bias():
        qi = lax.broadcasted_iota(jnp.int32, (BLOCK, n_keys), 0)
        kc = lax.broadcasted_iota(jnp.int32, (BLOCK, n_keys), 1)
        dist = jnp.abs(BLOCK + qi - kc)
        dist_f = dist.astype(F32)
        inside = dist <= WINDOW
        for variant in range(N_BIAS_VARIANTS):
            ok = inside
            if variant == 1:
                ok = ok & (kc >= BLOCK)
            if variant == 2:
                ok = ok & (kc < 2 * BLOCK)
            for j in range(n_slabs):
                slope = 2.0 ** -(_slab_head(j) + 1)
                bias_ref[variant, j * BLOCK:(j + 1) * BLOCK, :] = jnp.where(
                    ok, -slope * dist_f, MASKED_LOGIT)

    prev_start = pl.multiple_of(jnp.maximum(n - 1, 0) * BLOCK, BLOCK)
    cur_start = pl.multiple_of(n * BLOCK, BLOCK)
    next_start = pl.multiple_of(jnp.minimum(n + 1, n_blocks - 1) * BLOCK, BLOCK)
    starts = (prev_start, cur_start, next_start)
    k3 = jnp.concatenate([k_ref[0, pl.ds(s, BLOCK), :] for s in starts], axis=0)
    v3 = jnp.concatenate([v_ref[0, pl.ds(s, BLOCK), :] for s in starts], axis=0)
    variant = jnp.where(n == 0, 1, jnp.where(n == n_blocks - 1, 2, 0))

    q = q_ref[0]
    low = lax.broadcasted_iota(jnp.int32, (BLOCK, LANES), 1) < HEAD_DIM
    zero = jnp.zeros((BLOCK, LANES), BF16)
    slabs = []
    for g in range(D_ATTN // LANES):
        qg = q[:, g * LANES:(g + 1) * LANES]
        slabs.append(jnp.where(low, qg, zero))
        slabs.append(jnp.where(low, zero, qg))
    q_stack = jnp.concatenate(slabs, axis=0)
    scores = lax.dot_general(q_stack, k3, (((1,), (1,)), ((), ())),
                             preferred_element_type=F32)

    outs = []
    for j in range(n_slabs):
        sink = sink_ref[_slab_head(j)]
        logits = scores[j * BLOCK:(j + 1) * BLOCK] + bias_ref[variant, j * BLOCK:(j + 1) * BLOCK, :]
        m = jnp.maximum(jnp.max(logits, axis=-1, keepdims=True), sink)
        p = jnp.exp(logits - m)
        denom = jnp.sum(p, axis=-1, keepdims=True) + jnp.exp(sink - m)
        pv = jnp.dot(p.astype(BF16), v3, preferred_element_type=F32)
        outs.append(pv * (1.0 / denom))
    for g in range(D_ATTN // LANES):
        o_ref[0, :, g * LANES:(g + 1) * LANES] = jnp.where(low, outs[2 * g], outs[2 * g + 1])


def _attention(q, k, v, sink):
    batch, seq, _ = q.shape
    n_blocks = seq // BLOCK
    assert n_blocks >= 2
    return pl.pallas_call(
        _attn_kernel,
        out_shape=jax.ShapeDtypeStruct((batch, seq, D_ATTN), F32),
        grid=(batch, n_blocks),
        in_specs=[
            pl.BlockSpec(memory_space=pltpu.SMEM),
            pl.BlockSpec((1, BLOCK, D_ATTN), lambda b, n: (b, n, 0)),
            pl.BlockSpec((1, seq, KV_DIM), lambda b, n: (b, 0, 0)),
            pl.BlockSpec((1, seq, KV_DIM), lambda b, n: (b, 0, 0)),
        ],
        out_specs=pl.BlockSpec((1, BLOCK, D_ATTN), lambda b, n: (b, n, 0)),
        scratch_shapes=[pltpu.VMEM((N_BIAS_VARIANTS, N_Q_HEADS * BLOCK, N_KEY_BLOCKS * BLOCK), F32)],
        compiler_params=_compiler_params(2),
        name="banded_attention",
    )(sink, q, k, v)


def _shift_rows(x, d, fill, reverse):
    rows = x.shape[0]
    if d % SUBLANES == 0:
        pad = jnp.full((d, x.shape[1]), fill, x.dtype)
        if reverse:
            return jnp.concatenate([x[d:], pad], axis=0)
        return jnp.concatenate([pad, x[:rows - d]], axis=0)
    row = lax.broadcasted_iota(jnp.int32, x.shape, 0)
    if reverse:
        return jnp.where(row < rows - d, pltpu.roll(x, rows - d, axis=0), fill)
    return jnp.where(row >= d, pltpu.roll(x, d, axis=0), fill)


def _scan_rows(a, u, carry, reverse):
    rows = a.shape[0]
    d = 1
    while d < rows:
        u = a * _shift_rows(u, d, 0.0, reverse) + u
        a = a * _shift_rows(a, d, 1.0, reverse)
        d *= 2
    return a * carry + u


def _log_sigmoid(x):
    return jnp.minimum(x, 0.0) - jnp.log1p(jnp.exp(-jnp.abs(x)))


def _lru_kernel(xr_ref, gate_ref, wg_ref, p_ref, o_ref, xpad_ref, hf_ref, hb_ref):
    seq = xr_ref.shape[2]
    n_super = seq // LRU_SUPER
    halo = SUBLANES

    zeros = jnp.zeros((halo, LANES), F32)
    xpad_ref[0:halo, :] = zeros
    xpad_ref[seq + halo:seq + 2 * halo, :] = zeros

    def copy_body(c, carry):
        t0 = pl.multiple_of(c * LRU_SUPER, LRU_SUPER)
        xpad_ref[pl.ds(t0 + halo, LRU_SUPER), :] = xr_ref[0, 0, pl.ds(t0, LRU_SUPER), :]
        return carry

    lax.fori_loop(0, n_super, copy_body, 0)

    params = p_ref[0]
    conv_w = [params[P_CONV_W + j:P_CONV_W + j + 1] for j in range(CONV_WIDTH)]
    conv_b = params[P_CONV_B:P_CONV_B + 1]
    window = LRU_SUPER + 2 * halo

    def recurrence_inputs(t0, direction):
        xw = xpad_ref[pl.ds(t0, window), :]
        xc = conv_b
        for j in range(CONV_WIDTH):
            shift = (CONV_PAD_LEFT - j) % window
            tap = xw if shift == 0 else pltpu.roll(xw, shift, axis=0)
            xc = xc + tap[halo:halo + LRU_SUPER] * conv_w[j]
        gates = jnp.dot(xc.astype(BF16), wg_ref[0, :, 2 * LANES * direction:2 * LANES * (direction + 1)],
                        preferred_element_type=F32)
        b_rg = params[P_B_RG + 2 * direction:P_B_RG + 2 * direction + 1]
        b_ig = params[P_B_IG + 2 * direction:P_B_IG + 2 * direction + 1]
        lam = params[P_LAM + direction:P_LAM + direction + 1]
        r = jax.nn.sigmoid(gates[:, :LANES] + b_rg)
        i = jax.nn.sigmoid(gates[:, LANES:] + b_ig)
        a = jnp.exp(LRU_C * r * _log_sigmoid(lam))
        u = jnp.sqrt(jnp.maximum(1.0 - a * a, 0.0)) * (i * xc)
        return a, u

    n_sub = LRU_SUPER // LRU_SUB

    def scan_body(c, carries):
        carry_f, carry_b = carries
        tf = pl.multiple_of(c * LRU_SUPER, LRU_SUPER)
        tb = pl.multiple_of((n_super - 1 - c) * LRU_SUPER, LRU_SUPER)
        a_f, u_f = recurrence_inputs(tf, 0)
        a_b, u_b = recurrence_inputs(tb, 1)
        for s in range(n_sub):
            lo = s * LRU_SUB
            h = _scan_rows(a_f[lo:lo + LRU_SUB], u_f[lo:lo + LRU_SUB], carry_f, False)
            hf_ref[pl.ds(tf + lo, LRU_SUB), :] = h
            carry_f = h[LRU_SUB - 1:LRU_SUB]
        for s in reversed(range(n_sub)):
            lo = s * LRU_SUB
            h = _scan_rows(a_b[lo:lo + LRU_SUB], u_b[lo:lo + LRU_SUB], carry_b, True)
            hb_ref[pl.ds(tb + lo, LRU_SUB), :] = h
            carry_b = h[0:1]
        return carry_f, carry_b

    zero_row = jnp.zeros((1, LANES), F32)
    lax.fori_loop(0, n_super, scan_body, (zero_row, zero_row))

    def out_body(c, carry):
        t0 = pl.multiple_of(c * LRU_SUPER, LRU_SUPER)
        g = gate_ref[0, 0, pl.ds(t0, LRU_SUPER), :]
        cdf = 0.5 * (1.0 + jnp.tanh(np.sqrt(2.0 / np.pi).astype(np.float32) * (g + 0.044715 * (g * g * g))))
        h = hf_ref[pl.ds(t0, LRU_SUPER), :] + hb_ref[pl.ds(t0, LRU_SUPER), :]
        o_ref[0, 0, pl.ds(t0, LRU_SUPER), :] = h * (g * cdf)
        return carry

    lax.fori_loop(0, n_super, out_body, 0)


def _rg_lru(xr, gate, w_gates, params):
    batch, _, seq, _ = xr.shape
    slab = pl.BlockSpec((1, 1, seq, LANES), lambda b, c: (b, c, 0, 0))
    return pl.pallas_call(
        _lru_kernel,
        out_shape=jax.ShapeDtypeStruct((batch, LRU_TILES, seq, LANES), F32),
        grid=(batch, LRU_TILES),
        in_specs=[
            slab,
            slab,
            pl.BlockSpec((1, LANES, 2 * N_DIR * LANES), lambda b, c: (c, 0, 0)),
            pl.BlockSpec((1, P_ROWS, LANES), lambda b, c: (c, 0, 0)),
        ],
        out_specs=slab,
        scratch_shapes=[
            pltpu.VMEM((seq + 2 * SUBLANES, LANES), F32),
            pltpu.VMEM((seq, LANES), F32),
            pltpu.VMEM((seq, LANES), F32),
        ],
        compiler_params=_compiler_params(2),
        name="rg_lru",
    )(xr, gate, w_gates, params)


def _mix_ffn_kernel(x_ref, attn_ref, lru_ref, mod_ref, ga_ref, gl_ref, wo_ref, g2_ref, wfi_ref, wfo_ref,
                    gf_ref, o_ref, act_ref, *, final):
    x = x_ref[0]
    mod = mod_ref[0]
    attn = _rms(attn_ref[0]) * ga_ref[...]
    lru = jnp.concatenate([lru_ref[0, c] for c in range(LRU_TILES)], axis=-1)
    lru = _rms(lru) * gl_ref[...]
    mix = jnp.concatenate([attn, lru], axis=-1).astype(BF16)
    x = x + mod[2:3] * jnp.dot(mix, wo_ref[...], preferred_element_type=F32)

    h = (_rms(x) * (g2_ref[...] * (1.0 + mod[4:5])) + mod[3:4]).astype(BF16)
    for j in range(D_FF // FF_COL_TILE):
        lo = j * FF_COL_TILE
        g = jnp.dot(h, wfi_ref[:, lo:lo + FF_COL_TILE], preferred_element_type=F32)
        u = jnp.dot(h, wfi_ref[:, D_FF + lo:D_FF + lo + FF_COL_TILE], preferred_element_type=F32)
        act_ref[:, lo:lo + FF_COL_TILE] = ((g * jax.nn.sigmoid(g)) * u).astype(BF16)
    x = x + mod[5:6] * jnp.dot(act_ref[...], wfo_ref[...], preferred_element_type=F32)
    if final:
        x = _rms(x) * gf_ref[...]
    o_ref[0] = x


def _mix_ffn(x, attn, lru, mod, g_attn, g_lru, w_out, g_norm2, w_ffn_in, w_ffn_out, g_final, final):
    batch, seq, _ = x.shape
    tm = TOKEN_TILE
    tok = lambda b, i: (b, i, 0)
    const2 = lambda b, i: (0, 0)
    resident = pl.Buffered(1)
    return pl.pallas_call(
        functools.partial(_mix_ffn_kernel, final=final),
        out_shape=jax.ShapeDtypeStruct((batch, seq, D_MODEL), F32),
        grid=(batch, seq // tm),
        in_specs=[
            pl.BlockSpec((1, tm, D_MODEL), tok),
            pl.BlockSpec((1, tm, D_ATTN), tok),
            pl.BlockSpec((1, LRU_TILES, tm, LANES), lambda b, i: (b, 0, i, 0)),
            pl.BlockSpec((1, N_MOD, D_MODEL), lambda b, i: (b, 0, 0)),
            pl.BlockSpec((1, D_ATTN), const2),
            pl.BlockSpec((1, D_LRU), const2),
            pl.BlockSpec((D_ATTN + D_LRU, D_MODEL), const2, pipeline_mode=resident),
            pl.BlockSpec((1, D_MODEL), const2),
            pl.BlockSpec((D_MODEL, 2 * D_FF), const2, pipeline_mode=resident),
            pl.BlockSpec((D_FF, D_MODEL), const2, pipeline_mode=resident),
            pl.BlockSpec((1, D_MODEL), const2),
        ],
        out_specs=pl.BlockSpec((1, tm, D_MODEL), tok),
        scratch_shapes=[pltpu.VMEM((tm, D_FF), BF16)],
        compiler_params=_compiler_params(2),
        name="mix_ffn",
    )(x, attn, lru, mod, g_attn, g_lru, w_out, g_norm2, w_ffn_in, w_ffn_out, g_final)


def _paired_head_order():
    cols = []
    for g in range(Q_PER_KV):
        cols.append(np.arange(HEAD_DIM) + HEAD_DIM * g)
        cols.append(np.arange(HEAD_DIM) + HEAD_DIM * (Q_PER_KV + g))
    return np.concatenate(cols)


def _gate_weights(w_rg, w_ig):
    per_tile = LANES // LRU_BLOCK
    eye = jnp.eye(per_tile, dtype=F32)

    def dense(w):
        w = w.reshape(N_DIR, LRU_TILES, per_tile, LRU_BLOCK, LRU_BLOCK)
        full = w[:, :, :, :, None, :] * eye[None, None, :, None, :, None]
        return full.reshape(N_DIR, LRU_TILES, LANES, LANES)

    rg, ig = dense(w_rg), dense(w_ig)
    cols = jnp.concatenate([rg[0], ig[0], rg[1], ig[1]], axis=-1)
    return cols.astype(BF16)


def _lru_params(conv_w, conv_b, b_rg, b_ig, lam):
    rows = [conv_w, conv_b[None], b_rg[0:1], b_ig[0:1], b_rg[1:2], b_ig[1:2], lam]
    table = jnp.concatenate(rows, axis=0)
    table = jnp.pad(table, ((0, P_ROWS - table.shape[0]), (0, 0)))
    return table.reshape(P_ROWS, LRU_TILES, LANES).transpose(1, 0, 2)


def kernel(x_prompt, x_sample, c_prompt, c_sample, w_mod, b_mod, g_norm1, w_in, sink, conv_w, conv_b,
           w_rg, b_rg, w_ig, b_ig, lam, g_attn_out, g_lru_out, w_out, g_norm2, w_ffn_in, w_ffn_out, g_final):
    order = _paired_head_order()
    n_prompt = c_prompt.shape[0]
    mod_all = _modulation(jnp.concatenate([c_prompt, c_sample], axis=0), w_mod, b_mod)
    mod_all = mod_all.reshape(DEPTH, -1, N_MOD, D_MODEL)

    w_in_b = jnp.concatenate([w_in[:, :, :D_ATTN][:, :, order], w_in[:, :, D_ATTN:]], axis=-1).astype(BF16)
    w_out_b = jnp.concatenate([w_out[:, :D_ATTN][:, order], w_out[:, D_ATTN:]], axis=1).astype(BF16)
    g_attn = g_attn_out[:, order]
    w_ffn_in_b = w_ffn_in.astype(BF16)
    w_ffn_out_b = w_ffn_out.astype(BF16)

    xs = [x_prompt, x_sample]
    for l in range(DEPTH):
        w_gates = _gate_weights(w_rg[l], w_ig[l])
        params = _lru_params(conv_w[l], conv_b[l], b_rg[l], b_ig[l], lam[l])
        mods = [mod_all[l, :n_prompt], mod_all[l, n_prompt:]]
        for t in range(2):
            x = xs[t]
            q, k, v, xr, gate = _in_projection(x, mods[t], g_norm1[l][None], w_in_b[l])
            attn = _attention(q, k, v, sink[l])
            lru = _rg_lru(xr, gate, w_gates, params)
            xs[t] = _mix_ffn(x, attn, lru, mods[t], g_attn[l][None], g_lru_out[l][None], w_out_b[l],
                             g_norm2[l][None], w_ffn_in_b[l], w_ffn_out_b[l], g_final[None],
                             final=(l == DEPTH - 1))
    return (xs[0], xs[1])
```

```python
import functools

import jax
import jax.numpy as jnp
import numpy as np
from jax import lax
from jax.experimental import pallas as pl
from jax.experimental.pallas import tpu as pltpu

F32 = jnp.float32
BF16 = jnp.bfloat16

D_MODEL = 1024
DEPTH = 4
D_ATTN = 512
N_Q_HEADS = 8
N_KV_HEADS = 2
HEAD_DIM = 64
Q_PER_KV = N_Q_HEADS // N_KV_HEADS
KV_DIM = N_KV_HEADS * HEAD_DIM
WINDOW = 128
BLOCK = 128
D_LRU = 512
N_LRU_BLOCKS = 8
LRU_BLOCK = D_LRU // N_LRU_BLOCKS
CONV_WIDTH = 4
CONV_PAD_LEFT = 2
LRU_C = 8.0
N_DIR = 2
D_IN = D_ATTN + 2 * KV_DIM + 2 * D_LRU
D_FF = 2816
N_MOD = 6
EPS = 1e-6
LOG2_E = float(np.log2(np.e))

LANES = 128
SUBLANES = 8
VMEM_LIMIT_BYTES = 56 * 1024 * 1024

TOKEN_TILE = 512
MOD_COL_TILE = 1536
FF_COL_TILE = 256
ATTN_Q_TILE = 512
LRU_TILES = D_LRU // LANES
LRU_CHUNK = 256
LRU_SEGMENT_GROUPS = 2
LRU_SEGMENTS = SUBLANES * LRU_SEGMENT_GROUPS
LRU_SEGMENT_PAD = 8
LRU_SCAN_UNROLL = 8
SQRT_FLOOR = 1e-30
N_KEY_BLOCKS = 3
N_BIAS_VARIANTS = 3
MASKED_DISTANCE = -(2.0 ** 100)

P_CONV_W = 0
P_CONV_B = 4
P_B_GATES = 5
P_LAM = 9
P_ROWS = 16


def _rms(x):
    return x * lax.rsqrt(jnp.mean(x * x, axis=-1, keepdims=True) + EPS)


def _compiler_params(n_grid_dims):
    return pltpu.CompilerParams(
        dimension_semantics=("arbitrary",) * n_grid_dims,
        vmem_limit_bytes=VMEM_LIMIT_BYTES,
    )


def _mod_kernel(c_ref, w_ref, b_ref, o_ref):
    c = c_ref[...]
    c_act = c * jax.nn.sigmoid(c)
    o_ref[0] = jnp.dot(c_act, w_ref[0], preferred_element_type=F32) + b_ref[0]


def _modulation(c_all, w_mod, b_mod):
    n_rows = c_all.shape[0]
    n_cols = N_MOD * D_MODEL
    return pl.pallas_call(
        _mod_kernel,
        out_shape=jax.ShapeDtypeStruct((DEPTH, n_rows, n_cols), F32),
        grid=(DEPTH, n_cols // MOD_COL_TILE),
        in_specs=[
            pl.BlockSpec((n_rows, D_MODEL), lambda l, j: (0, 0)),
            pl.BlockSpec((1, D_MODEL, MOD_COL_TILE), lambda l, j: (l, 0, j)),
            pl.BlockSpec((1, 1, MOD_COL_TILE), lambda l, j: (l, 0, j)),
        ],
        out_specs=pl.BlockSpec((1, n_rows, MOD_COL_TILE), lambda l, j: (l, 0, j)),
        compiler_params=_compiler_params(2),
        name="modulation",
    )(c_all, w_mod, b_mod.reshape(DEPTH, 1, n_cols))


def _inproj_kernel(x_ref, mod_ref, g_ref, w_ref, q_ref, k_ref, v_ref, xr_ref, gate_ref):
    x = x_ref[0]
    mod = mod_ref[0]
    h = _rms(x) * (g_ref[...] * (1.0 + mod[1:2])) + mod[0:1]
    z = jnp.dot(h.astype(BF16), w_ref[...], preferred_element_type=F32)
    q_ref[0] = (z[:, :D_ATTN] * (HEAD_DIM ** -0.5)).astype(BF16)
    k_ref[0] = z[:, D_ATTN:D_ATTN + KV_DIM].astype(BF16)
    v_ref[0] = z[:, D_ATTN + KV_DIM:D_ATTN + 2 * KV_DIM].astype(BF16)
    base = D_ATTN + 2 * KV_DIM
    for c in range(LRU_TILES):
        xr_ref[0, c] = z[:, base + c * LANES:base + (c + 1) * LANES]
        gate_ref[0, c] = z[:, base + D_LRU + c * LANES:base + D_LRU + (c + 1) * LANES]


def _in_projection(x, mod, g_norm1, w_in):
    batch, seq, _ = x.shape
    tm = TOKEN_TILE
    tok = lambda b, i: (b, i, 0)
    lru_spec = pl.BlockSpec((1, LRU_TILES, tm, LANES), lambda b, i: (b, 0, i, 0))
    return pl.pallas_call(
        _inproj_kernel,
        out_shape=(
            jax.ShapeDtypeStruct((batch, seq, D_ATTN), BF16),
            jax.ShapeDtypeStruct((batch, seq, KV_DIM), BF16),
            jax.ShapeDtypeStruct((batch, seq, KV_DIM), BF16),
            jax.ShapeDtypeStruct((batch, LRU_TILES, seq, LANES), F32),
            jax.ShapeDtypeStruct((batch, LRU_TILES, seq, LANES), F32),
        ),
        grid=(batch, seq // tm),
        in_specs=[
            pl.BlockSpec((1, tm, D_MODEL), tok),
            pl.BlockSpec((1, N_MOD, D_MODEL), lambda b, i: (b, 0, 0)),
            pl.BlockSpec((1, D_MODEL), lambda b, i: (0, 0)),
            pl.BlockSpec((D_MODEL, D_IN), lambda b, i: (0, 0)),
        ],
        out_specs=(
            pl.BlockSpec((1, tm, D_ATTN), tok),
            pl.BlockSpec((1, tm, KV_DIM), tok),
            pl.BlockSpec((1, tm, KV_DIM), tok),
            lru_spec,
            lru_spec,
        ),
        compiler_params=_compiler_params(2),
        name="in_projection",
    )(x, mod, g_norm1, w_in)


def _slab_head(j):
    return j // 2 + Q_PER_KV * (j % 2)


def _attn_kernel(sink_ref, q_ref, k_ref, v_ref, o_ref, dist_ref, slope_ref):
    tile = pl.program_id(1)
    blocks_per_tile = ATTN_Q_TILE // BLOCK
    n_blocks = pl.num_programs(1) * blocks_per_tile
    n_slabs = N_Q_HEADS
    n_keys = N_KEY_BLOCKS * BLOCK

    @pl.when((pl.program_id(0) == 0) & (tile == 0))
    def _init_tables():
        kc = lax.broadcasted_iota(jnp.int32, (n_keys, BLOCK), 0)
        qi = lax.broadcasted_iota(jnp.int32, (n_keys, BLOCK), 1)
        dist = jnp.abs(BLOCK + qi - kc)
        neg_dist = -dist.astype(F32)
        inside = dist <= WINDOW
        for variant in range(N_BIAS_VARIANTS):
            ok = inside
            if variant == 1:
                ok = ok & (kc >= BLOCK)
            if variant == 2:
                ok = ok & (kc < 2 * BLOCK)
            dist_ref[variant] = jnp.where(ok, neg_dist, MASKED_DISTANCE).astype(BF16)
        row = lax.broadcasted_iota(jnp.int32, (BLOCK, LANES), 0)
        col = lax.broadcasted_iota(jnp.int32, (BLOCK, LANES), 1)
        for j in range(n_slabs):
            slope = 2.0 ** -(_slab_head(j) + 1)
            slope_ref[j * BLOCK:(j + 1) * BLOCK, :] = jnp.where(row == col, slope, 0.0).astype(BF16)

    low = lax.broadcasted_iota(jnp.int32, (BLOCK, LANES), 1) < HEAD_DIM
    zero = jnp.zeros((BLOCK, LANES), BF16)
    ones = jnp.ones((n_keys, LANES), BF16)

    for r in range(blocks_per_tile):
        n = tile * blocks_per_tile + r
        prev_start = pl.multiple_of(jnp.maximum(n - 1, 0) * BLOCK, BLOCK)
        cur_start = pl.multiple_of(n * BLOCK, BLOCK)
        next_start = pl.multiple_of(jnp.minimum(n + 1, n_blocks - 1) * BLOCK, BLOCK)
        starts = (prev_start, cur_start, next_start)
        k3 = jnp.concatenate([k_ref[0, pl.ds(s, BLOCK), :] for s in starts], axis=0)
        v3 = jnp.concatenate([v_ref[0, pl.ds(s, BLOCK), :] for s in starts], axis=0)
        variant = jnp.where(n == 0, 1, jnp.where(n == n_blocks - 1, 2, 0))
        k_aug = jnp.concatenate([k3, dist_ref[variant]], axis=1)
        v_aug = jnp.concatenate([v3, ones], axis=1)

        q = q_ref[0, r * BLOCK:(r + 1) * BLOCK, :]
        slabs = []
        for g in range(D_ATTN // LANES):
            qg = q[:, g * LANES:(g + 1) * LANES]
            slabs.append(jnp.where(low, qg, zero))
            slabs.append(jnp.where(low, zero, qg))
        q_aug = jnp.concatenate([jnp.concatenate(slabs, axis=0), slope_ref[...]], axis=1)
        logits_all = lax.dot_general(q_aug, k_aug, (((1,), (1,)), ((), ())),
                                     preferred_element_type=F32)

        outs = []
        for j in range(n_slabs):
            sink = sink_ref[_slab_head(j)]
            logits = logits_all[j * BLOCK:(j + 1) * BLOCK]
            m = jnp.maximum(jnp.max(logits, axis=-1, keepdims=True), sink)
            p = jnp.exp(logits - m).astype(BF16)
            pv = jnp.dot(p, v_aug, preferred_element_type=F32)
            denom = pv[:, LANES:] + jnp.exp(sink - m)
            outs.append(pv[:, :LANES] * (1.0 / denom))
        for g in range(D_ATTN // LANES):
            o_ref[0, r * BLOCK:(r + 1) * BLOCK, g * LANES:(g + 1) * LANES] = jnp.where(
                low, outs[2 * g], outs[2 * g + 1])


def _attention(q, k, v, sink):
    batch, seq, _ = q.shape
    tq = ATTN_Q_TILE
    assert seq % tq == 0 and seq // BLOCK >= 2
    return pl.pallas_call(
        _attn_kernel,
        out_shape=jax.ShapeDtypeStruct((batch, seq, D_ATTN), F32),
        grid=(batch, seq // tq),
        in_specs=[
            pl.BlockSpec(memory_space=pltpu.SMEM),
            pl.BlockSpec((1, tq, D_ATTN), lambda b, n: (b, n, 0)),
            pl.BlockSpec((1, seq, KV_DIM), lambda b, n: (b, 0, 0)),
            pl.BlockSpec((1, seq, KV_DIM), lambda b, n: (b, 0, 0)),
        ],
        out_specs=pl.BlockSpec((1, tq, D_ATTN), lambda b, n: (b, n, 0)),
        scratch_shapes=[
            pltpu.VMEM((N_BIAS_VARIANTS, N_KEY_BLOCKS * BLOCK, LANES), BF16),
            pltpu.VMEM((N_Q_HEADS * BLOCK, LANES), BF16),
        ],
        compiler_params=_compiler_params(2),
        name="banded_attention",
    )(sink, q, k, v)


def _log_sigmoid(x):
    return jnp.minimum(x, 0.0) - jnp.log1p(jnp.exp(-jnp.abs(x)))


def _lru_kernel(xr_ref, wg_ref, p_ref, o_ref, xpad_ref, af_ref, uf_ref, ab_ref, ub_ref, hf_ref, hb_ref):
    seq = xr_ref.shape[2]
    n_chunks = seq // LRU_CHUNK
    seg_len = seq // LRU_SEGMENTS
    pitch = seg_len + LRU_SEGMENT_PAD
    piece = min(LRU_CHUNK, seg_len)
    halo = SUBLANES

    def scan_row(t):
        return pl.multiple_of(t + (t // seg_len) * LRU_SEGMENT_PAD, SUBLANES)

    zeros = jnp.zeros((halo, LANES), F32)
    xpad_ref[0:halo, :] = zeros
    xpad_ref[seq + halo:seq + 2 * halo, :] = zeros

    def copy_body(c, carry):
        t0 = pl.multiple_of(c * LRU_CHUNK, LRU_CHUNK)
        xpad_ref[pl.ds(t0 + halo, LRU_CHUNK), :] = xr_ref[0, 0, pl.ds(t0, LRU_CHUNK), :]
        return carry

    lax.fori_loop(0, n_chunks, copy_body, 0)

    params = p_ref[0]
    conv_w = [params[P_CONV_W + j:P_CONV_W + j + 1] for j in range(CONV_WIDTH)]
    conv_b = params[P_CONV_B:P_CONV_B + 1]
    b_gates = jnp.concatenate([params[P_B_GATES + g:P_B_GATES + g + 1] for g in range(2 * N_DIR)], axis=1)
    rates = [(0.5 * LRU_C * LOG2_E) * _log_sigmoid(params[P_LAM + d:P_LAM + d + 1]) for d in range(N_DIR)]
    a_refs = (af_ref, ab_ref)
    u_refs = (uf_ref, ub_ref)

    def inputs_body(c, carry):
        t0 = pl.multiple_of(c * LRU_CHUNK, LRU_CHUNK)
        xc = conv_b
        for j in range(CONV_WIDTH):
            xc = xc + xpad_ref[pl.ds(t0 + halo - CONV_PAD_LEFT + j, LRU_CHUNK), :] * conv_w[j]
        t = jnp.tanh(jnp.dot(xc.astype(BF16), wg_ref[0], preferred_element_type=F32) + b_gates)
        x_half = 0.5 * xc
        for d in range(N_DIR):
            t_r = t[:, 2 * LANES * d:2 * LANES * d + LANES]
            t_i = t[:, 2 * LANES * d + LANES:2 * LANES * (d + 1)]
            a = jnp.exp2(rates[d] + rates[d] * t_r)
            s = jnp.maximum(1.0 - a * a, 0.0)
            u = (s * lax.rsqrt(jnp.maximum(s, SQRT_FLOOR))) * (x_half + x_half * t_i)
            for lo in range(0, LRU_CHUNK, piece):
                row0 = scan_row(t0 + lo)
                a_refs[d][pl.ds(row0, piece), :] = a[lo:lo + piece]
                u_refs[d][pl.ds(row0, piece), :] = u[lo:lo + piece]
        return carry

    lax.fori_loop(0, n_chunks, inputs_body, 0, unroll=2)

    def segment_rows(k, group):
        return pl.ds(group * SUBLANES * pitch + k, SUBLANES, stride=pitch)

    groups = range(LRU_SEGMENT_GROUPS)
    n_steps = seg_len // LRU_SCAN_UNROLL

    def summary_body(s, carry):
        prod_f, h_f, prod_b, h_b = [list(v) for v in carry]
        for i in range(LRU_SCAN_UNROLL):
            kf = s * LRU_SCAN_UNROLL + i
            kb = seg_len - 1 - kf
            for g in groups:
                a = af_ref[segment_rows(kf, g), :]
                h_f[g] = a * h_f[g] + uf_ref[segment_rows(kf, g), :]
                prod_f[g] = prod_f[g] * a
                a = ab_ref[segment_rows(kb, g), :]
                h_b[g] = a * h_b[g] + ub_ref[segment_rows(kb, g), :]
                prod_b[g] = prod_b[g] * a
        return tuple(prod_f), tuple(h_f), tuple(prod_b), tuple(h_b)

    ones = tuple(jnp.ones((SUBLANES, LANES), F32) for _ in groups)
    zeros_g = tuple(jnp.zeros((SUBLANES, LANES), F32) for _ in groups)
    prod_f, end_f, prod_b, end_b = lax.fori_loop(0, n_steps, summary_body, (ones, zeros_g, ones, zeros_g))
    prod_f, end_f, prod_b, end_b = [jnp.concatenate(v, axis=0) for v in (prod_f, end_f, prod_b, end_b)]

    entry = jnp.zeros((1, LANES), F32)
    entries_f = [entry]
    for j in range(1, LRU_SEGMENTS):
        entry = prod_f[j - 1:j] * entry + end_f[j - 1:j]
        entries_f.append(entry)
    entry = jnp.zeros((1, LANES), F32)
    entries_b = [entry]
    for j in range(LRU_SEGMENTS - 2, -1, -1):
        entry = prod_b[j + 1:j + 2] * entry + end_b[j + 1:j + 2]
        entries_b.append(entry)
    entries_b = entries_b[::-1]
    start_f = tuple(jnp.concatenate(entries_f[g * SUBLANES:(g + 1) * SUBLANES], axis=0) for g in groups)
    start_b = tuple(jnp.concatenate(entries_b[g * SUBLANES:(g + 1) * SUBLANES], axis=0) for g in groups)

    def scan_body(s, carry):
        h_f, h_b = [list(v) for v in carry]
        for i in range(LRU_SCAN_UNROLL):
            kf = s * LRU_SCAN_UNROLL + i
            kb = seg_len - 1 - kf
            for g in groups:
                h_f[g] = af_ref[segment_rows(kf, g), :] * h_f[g] + uf_ref[segment_rows(kf, g), :]
                hf_ref[segment_rows(kf, g), :] = h_f[g]
                h_b[g] = ab_ref[segment_rows(kb, g), :] * h_b[g] + ub_ref[segment_rows(kb, g), :]
                hb_ref[segment_rows(kb, g), :] = h_b[g]
        return tuple(h_f), tuple(h_b)

    lax.fori_loop(0, n_steps, scan_body, (start_f, start_b))

    def out_body(c, carry):
        t0 = pl.multiple_of(c * LRU_CHUNK, LRU_CHUNK)
        for lo in range(0, LRU_CHUNK, piece):
            row0 = scan_row(t0 + lo)
            o_ref[0, 0, pl.ds(t0 + lo, piece), :] = hf_ref[pl.ds(row0, piece), :] + hb_ref[pl.ds(row0, piece), :]
        return carry

    lax.fori_loop(0, n_chunks, out_body, 0)


def _rg_lru(xr, w_gates, params):
    batch, _, seq, _ = xr.shape
    seg_len = seq // LRU_SEGMENTS
    assert seq % LRU_CHUNK == 0 and seg_len % LRU_SCAN_UNROLL == 0
    assert LRU_CHUNK % seg_len == 0 or seg_len % LRU_CHUNK == 0
    scan_rows = LRU_SEGMENTS * (seg_len + LRU_SEGMENT_PAD)
    slab = pl.BlockSpec((1, 1, seq, LANES), lambda b, c: (b, c, 0, 0))
    return pl.pallas_call(
        _lru_kernel,
        out_shape=jax.ShapeDtypeStruct((batch, LRU_TILES, seq, LANES), F32),
        grid=(batch, LRU_TILES),
        in_specs=[
            slab,
            pl.BlockSpec((1, LANES, 2 * N_DIR * LANES), lambda b, c: (c, 0, 0)),
            pl.BlockSpec((1, P_ROWS, LANES), lambda b, c: (c, 0, 0)),
        ],
        out_specs=slab,
        scratch_shapes=[pltpu.VMEM((seq + 2 * SUBLANES, LANES), F32)]
        + [pltpu.VMEM((scan_rows, LANES), F32) for _ in range(3 * N_DIR)],
        compiler_params=_compiler_params(2),
        name="rg_lru",
    )(xr, w_gates, params)


def _gelu_tanh(x):
    cdf = 0.5 * (1.0 + jnp.tanh(float(np.sqrt(2.0 / np.pi)) * (x + 0.044715 * (x * x * x))))
    return x * cdf


def _mix_ffn_kernel(x_ref, attn_ref, lru_ref, gate_ref, mod_ref, ga_ref, gl_ref, wo_ref, g2_ref, wfi_ref,
                    wfo_ref, gf_ref, o_ref, act_ref, *, final):
    x = x_ref[0]
    mod = mod_ref[0]
    attn = _rms(attn_ref[0]) * ga_ref[...]
    lru = jnp.concatenate([lru_ref[0, c] * _gelu_tanh(gate_ref[0, c]) for c in range(LRU_TILES)], axis=-1)
    lru = _rms(lru) * gl_ref[...]
    mix = jnp.concatenate([attn, lru], axis=-1).astype(BF16)
    x = x + mod[2:3] * jnp.dot(mix, wo_ref[...], preferred_element_type=F32)

    h = (_rms(x) * (g2_ref[...] * (1.0 + mod[4:5])) + mod[3:4]).astype(BF16)
    for j in range(D_FF // FF_COL_TILE):
        lo = j * FF_COL_TILE
        g = jnp.dot(h, wfi_ref[:, lo:lo + FF_COL_TILE], preferred_element_type=F32)
        u = jnp.dot(h, wfi_ref[:, D_FF + lo:D_FF + lo + FF_COL_TILE], preferred_element_type=F32)
        act_ref[:, lo:lo + FF_COL_TILE] = ((g * jax.nn.sigmoid(g)) * u).astype(BF16)
    x = x + mod[5:6] * jnp.dot(act_ref[...], wfo_ref[...], preferred_element_type=F32)
    if final:
        x = _rms(x) * gf_ref[...]
    o_ref[0] = x


def _mix_ffn(x, attn, lru, gate, mod, g_attn, g_lru, w_out, g_norm2, w_ffn_in, w_ffn_out, g_final, final):
    batch, seq, _ = x.shape
    tm = TOKEN_TILE
    tok = lambda b, i: (b, i, 0)
    const2 = lambda b, i: (0, 0)
    lru_spec = pl.BlockSpec((1, LRU_TILES, tm, LANES), lambda b, i: (b, 0, i, 0))
    resident = pl.Buffered(1)
    return pl.pallas_call(
        functools.partial(_mix_ffn_kernel, final=final),
        out_shape=jax.ShapeDtypeStruct((batch, seq, D_MODEL), F32),
        grid=(batch, seq // tm),
        in_specs=[
            pl.BlockSpec((1, tm, D_MODEL), tok),
            pl.BlockSpec((1, tm, D_ATTN), tok),
            lru_spec,
            lru_spec,
            pl.BlockSpec((1, N_MOD, D_MODEL), lambda b, i: (b, 0, 0)),
            pl.BlockSpec((1, D_ATTN), const2),
            pl.BlockSpec((1, D_LRU), const2),
            pl.BlockSpec((D_ATTN + D_LRU, D_MODEL), const2, pipeline_mode=resident),
            pl.BlockSpec((1, D_MODEL), const2),
            pl.BlockSpec((D_MODEL, 2 * D_FF), const2, pipeline_mode=resident),
            pl.BlockSpec((D_FF, D_MODEL), const2, pipeline_mode=resident),
            pl.BlockSpec((1, D_MODEL), const2),
        ],
        out_specs=pl.BlockSpec((1, tm, D_MODEL), tok),
        scratch_shapes=[pltpu.VMEM((tm, D_FF), BF16)],
        compiler_params=_compiler_params(2),
        name="mix_ffn",
    )(x, attn, lru, gate, mod, g_attn, g_lru, w_out, g_norm2, w_ffn_in, w_ffn_out, g_final)


def _paired_head_order():
    cols = []
    for g in range(Q_PER_KV):
        cols.append(np.arange(HEAD_DIM) + HEAD_DIM * g)
        cols.append(np.arange(HEAD_DIM) + HEAD_DIM * (Q_PER_KV + g))
    return np.concatenate(cols)


def _gate_weights(w_rg, w_ig):
    per_tile = LANES // LRU_BLOCK
    eye = jnp.eye(per_tile, dtype=F32)

    def dense(w):
        w = w.reshape(N_DIR, LRU_TILES, per_tile, LRU_BLOCK, LRU_BLOCK)
        full = w[:, :, :, :, None, :] * eye[None, None, :, None, :, None]
        return full.reshape(N_DIR, LRU_TILES, LANES, LANES)

    rg, ig = dense(w_rg), dense(w_ig)
    cols = jnp.concatenate([rg[0], ig[0], rg[1], ig[1]], axis=-1)
    return cols.astype(BF16) * 0.5


def _lru_params(conv_w, conv_b, b_rg, b_ig, lam):
    rows = [conv_w, conv_b[None], 0.5 * b_rg[0:1], 0.5 * b_ig[0:1], 0.5 * b_rg[1:2], 0.5 * b_ig[1:2], lam]
    table = jnp.concatenate(rows, axis=0)
    table = jnp.pad(table, ((0, P_ROWS - table.shape[0]), (0, 0)))
    return table.reshape(P_ROWS, LRU_TILES, LANES).transpose(1, 0, 2)


def kernel(x_prompt, x_sample, c_prompt, c_sample, w_mod, b_mod, g_norm1, w_in, sink, conv_w, conv_b,
           w_rg, b_rg, w_ig, b_ig, lam, g_attn_out, g_lru_out, w_out, g_norm2, w_ffn_in, w_ffn_out, g_final):
    order = _paired_head_order()
    n_prompt = c_prompt.shape[0]
    mod_all = _modulation(jnp.concatenate([c_prompt, c_sample], axis=0), w_mod, b_mod)
    mod_all = mod_all.reshape(DEPTH, -1, N_MOD, D_MODEL)

    w_in_b = jnp.concatenate([w_in[:, :, :D_ATTN][:, :, order], w_in[:, :, D_ATTN:]], axis=-1).astype(BF16)
    w_out_b = jnp.concatenate([w_out[:, :D_ATTN][:, order], w_out[:, D_ATTN:]], axis=1).astype(BF16)
    g_attn = g_attn_out[:, order]
    w_ffn_in_b = w_ffn_in.astype(BF16)
    w_ffn_out_b = w_ffn_out.astype(BF16)

    xs = [x_prompt, x_sample]
    for l in range(DEPTH):
        w_gates = _gate_weights(w_rg[l], w_ig[l])
        params = _lru_params(conv_w[l], conv_b[l], b_rg[l], b_ig[l], lam[l])
        mods = [mod_all[l, :n_prompt], mod_all[l, n_prompt:]]
        for t in range(2):
            x = xs[t]
            q, k, v, xr, gate = _in_projection(x, mods[t], g_norm1[l][None], w_in_b[l])
            attn = _attention(q, k, v, sink[l])
            lru = _rg_lru(xr, w_gates, params)
            xs[t] = _mix_ffn(x, attn, lru, gate, mods[t], g_attn[l][None], g_lru_out[l][None], w_out_b[l],
                             g_norm2[l][None], w_ffn_in_b[l], w_ffn_out_b[l], g_final[None],
                             final=(l == DEPTH - 1))
    return (xs[0], xs[1])
```

```python
import functools

import jax
import jax.numpy as jnp
import numpy as np
from jax import lax
from jax.experimental import pallas as pl
from jax.experimental.pallas import tpu as pltpu

F32 = jnp.float32
BF16 = jnp.bfloat16

D_MODEL = 1024
DEPTH = 4
D_ATTN = 512
N_Q_HEADS = 8
N_KV_HEADS = 2
HEAD_DIM = 64
Q_PER_KV = N_Q_HEADS // N_KV_HEADS
KV_DIM = N_KV_HEADS * HEAD_DIM
WINDOW = 128
BLOCK = 128
D_LRU = 512
N_LRU_BLOCKS = 8
LRU_BLOCK = D_LRU // N_LRU_BLOCKS
CONV_WIDTH = 4
CONV_PAD_LEFT = 2
LRU_C = 8.0
N_DIR = 2
D_IN = D_ATTN + 2 * KV_DIM + 2 * D_LRU
D_FF = 2816
N_MOD = 6
EPS = 1e-6
LOG2_E = float(np.log2(np.e))

LANES = 128
SUBLANES = 8
VMEM_LIMIT_BYTES = 56 * 1024 * 1024

TOKEN_TILE = 512
INPROJ_TILE = 1024
MOD_COL_TILE = 1536
FF_COL_TILE = 256
LRU_TILES = D_LRU // LANES
LRU_CHUNK = 256
LRU_CHUNKS_PER_BLOCK = 2
LRU_SEGMENT_GROUPS = 4
LRU_SEGMENTS = SUBLANES * LRU_SEGMENT_GROUPS
LRU_SEGMENT_PAD = 4
LRU_SCAN_UNROLL = 4
SQRT_FLOOR = 1e-30
BIAS_SPLIT = 3
N_KEY_BLOCKS = 3
N_BIAS_VARIANTS = 3
MASKED_DISTANCE = -(2.0 ** 100)

P_CONV_W = 0
P_CONV_B = 4
P_LAM = 5
P_ROWS = 8


def _rms(x):
    return x * lax.rsqrt(jnp.mean(x * x, axis=-1, keepdims=True) + EPS)


def _compiler_params(n_grid_dims):
    return pltpu.CompilerParams(
        dimension_semantics=("arbitrary",) * n_grid_dims,
        vmem_limit_bytes=VMEM_LIMIT_BYTES,
    )


def _mod_kernel(c_ref, w_ref, b_ref, o_ref):
    c = c_ref[...]
    c_act = c * jax.nn.sigmoid(c)
    o_ref[0] = jnp.dot(c_act, w_ref[0], preferred_element_type=F32) + b_ref[0]


def _modulation(c_all, w_mod, b_mod):
    n_rows = c_all.shape[0]
    n_cols = N_MOD * D_MODEL
    return pl.pallas_call(
        _mod_kernel,
        out_shape=jax.ShapeDtypeStruct((DEPTH, n_rows, n_cols), F32),
        grid=(DEPTH, n_cols // MOD_COL_TILE),
        in_specs=[
            pl.BlockSpec((n_rows, D_MODEL), lambda l, j: (0, 0)),
            pl.BlockSpec((1, D_MODEL, MOD_COL_TILE), lambda l, j: (l, 0, j)),
            pl.BlockSpec((1, 1, MOD_COL_TILE), lambda l, j: (l, 0, j)),
        ],
        out_specs=pl.BlockSpec((1, n_rows, MOD_COL_TILE), lambda l, j: (l, 0, j)),
        compiler_params=_compiler_params(2),
        name="modulation",
    )(c_all, w_mod, b_mod.reshape(DEPTH, 1, n_cols))


def _inproj_kernel(x_ref, mod_ref, g_ref, w_ref, q_ref, k_ref, v_ref, xr_ref, gate_ref):
    x = x_ref[0]
    mod = mod_ref[0]
    h = _rms(x) * (g_ref[...] * (1.0 + mod[1:2])) + mod[0:1]
    z = jnp.dot(h.astype(BF16), w_ref[...], preferred_element_type=F32)
    q_ref[0] = (z[:, :D_ATTN] * (HEAD_DIM ** -0.5)).astype(BF16)
    k_ref[0] = z[:, D_ATTN:D_ATTN + KV_DIM].astype(BF16)
    v_ref[0] = z[:, D_ATTN + KV_DIM:D_ATTN + 2 * KV_DIM].astype(BF16)
    base = D_ATTN + 2 * KV_DIM
    for c in range(LRU_TILES):
        xr_ref[0, c] = z[:, base + c * LANES:base + (c + 1) * LANES]
        gate_ref[0, c] = z[:, base + D_LRU + c * LANES:base + D_LRU + (c + 1) * LANES]


def _in_projection(x, mod, g_norm1, w_in):
    batch, seq, _ = x.shape
    tm = INPROJ_TILE
    tok = lambda b, i: (b, i, 0)
    lru_spec = pl.BlockSpec((1, LRU_TILES, tm, LANES), lambda b, i: (b, 0, i, 0))
    return pl.pallas_call(
        _inproj_kernel,
        out_shape=(
            jax.ShapeDtypeStruct((batch, seq, D_ATTN), BF16),
            jax.ShapeDtypeStruct((batch, seq, KV_DIM), BF16),
            jax.ShapeDtypeStruct((batch, seq, KV_DIM), BF16),
            jax.ShapeDtypeStruct((batch, LRU_TILES, seq, LANES), F32),
            jax.ShapeDtypeStruct((batch, LRU_TILES, seq, LANES), F32),
        ),
        grid=(batch, seq // tm),
        in_specs=[
            pl.BlockSpec((1, tm, D_MODEL), tok),
            pl.BlockSpec((1, N_MOD, D_MODEL), lambda b, i: (b, 0, 0)),
            pl.BlockSpec((1, D_MODEL), lambda b, i: (0, 0)),
            pl.BlockSpec((D_MODEL, D_IN), lambda b, i: (0, 0)),
        ],
        out_specs=(
            pl.BlockSpec((1, tm, D_ATTN), tok),
            pl.BlockSpec((1, tm, KV_DIM), tok),
            pl.BlockSpec((1, tm, KV_DIM), tok),
            lru_spec,
            lru_spec,
        ),
        compiler_params=_compiler_params(2),
        name="in_projection",
    )(x, mod, g_norm1, w_in)


def _slab_head(j):
    return j // 2 + Q_PER_KV * (j % 2)


def _init_attention_tables(dist_ref, slope_ref):
    n_keys = N_KEY_BLOCKS * BLOCK
    kc = lax.broadcasted_iota(jnp.int32, (n_keys, BLOCK), 0)
    qi = lax.broadcasted_iota(jnp.int32, (n_keys, BLOCK), 1)
    dist = jnp.abs(BLOCK + qi - kc)
    neg_dist = -dist.astype(F32)
    inside = dist <= WINDOW
    for variant in range(N_BIAS_VARIANTS):
        ok = inside
        if variant == 1:
            ok = ok & (kc >= BLOCK)
        if variant == 2:
            ok = ok & (kc < 2 * BLOCK)
        dist_ref[variant] = jnp.where(ok, neg_dist, MASKED_DISTANCE).astype(BF16)
    row = lax.broadcasted_iota(jnp.int32, (BLOCK, LANES), 0)
    col = lax.broadcasted_iota(jnp.int32, (BLOCK, LANES), 1)
    for j in range(N_Q_HEADS):
        slope = 2.0 ** -(_slab_head(j) + 1)
        slope_ref[j * BLOCK:(j + 1) * BLOCK, :] = jnp.where(row == col, slope, 0.0).astype(BF16)


def _attention_block(n, n_blocks, q, sink_ref, k_ref, v_ref, dist_ref, slope_ref):
    low = lax.broadcasted_iota(jnp.int32, (BLOCK, LANES), 1) < HEAD_DIM
    zero = jnp.zeros((BLOCK, LANES), BF16)
    ones = jnp.ones((N_KEY_BLOCKS * BLOCK, LANES), BF16)
    prev_start = pl.multiple_of(jnp.maximum(n - 1, 0) * BLOCK, BLOCK)
    cur_start = pl.multiple_of(n * BLOCK, BLOCK)
    next_start = pl.multiple_of(jnp.minimum(n + 1, n_blocks - 1) * BLOCK, BLOCK)
    starts = (prev_start, cur_start, next_start)
    k3 = jnp.concatenate([k_ref[0, pl.ds(s, BLOCK), :] for s in starts], axis=0)
    v3 = jnp.concatenate([v_ref[0, pl.ds(s, BLOCK), :] for s in starts], axis=0)
    variant = jnp.where(n == 0, 1, jnp.where(n == n_blocks - 1, 2, 0))
    k_aug = jnp.concatenate([k3, dist_ref[variant]], axis=1)
    v_aug = jnp.concatenate([v3, ones], axis=1)

    slabs = []
    for g in range(D_ATTN // LANES):
        qg = q[:, g * LANES:(g + 1) * LANES]
        slabs.append(jnp.where(low, qg, zero))
        slabs.append(jnp.where(low, zero, qg))
    q_aug = jnp.concatenate([jnp.concatenate(slabs, axis=0), slope_ref[...]], axis=1)
    logits_all = lax.dot_general(q_aug, k_aug, (((1,), (1,)), ((), ())),
                                 preferred_element_type=F32)
    outs = []
    for j in range(N_Q_HEADS):
        sink = sink_ref[_slab_head(j)]
        logits = logits_all[j * BLOCK:(j + 1) * BLOCK]
        m = jnp.maximum(jnp.max(logits, axis=-1, keepdims=True), sink)
        p = jnp.exp(logits - m).astype(BF16)
        pv = jnp.dot(p, v_aug, preferred_element_type=F32)
        denom = pv[:, LANES:] + jnp.exp(sink - m)
        outs.append(pv[:, :LANES] * (1.0 / denom))
    return [jnp.where(low, outs[2 * g], outs[2 * g + 1]) for g in range(D_ATTN // LANES)]


def _log_sigmoid(x):
    return jnp.minimum(x, 0.0) - jnp.log1p(jnp.exp(-jnp.abs(x)))


def _attn_lru_kernel(sink_ref, q_ref, k_ref, v_ref, xr_ref, wg_ref, p_ref, attn_ref, h_ref,
                     dist_ref, slope_ref, xpad_ref, af_ref, uf_ref, ab_ref, ub_ref, hf_ref, hb_ref):
    seq = xr_ref.shape[2]
    n_blocks = seq // BLOCK
    blocks_per_step = q_ref.shape[1] // BLOCK
    n_chunks = seq // LRU_CHUNK
    seg_len = seq // LRU_SEGMENTS
    seg_shift = seg_len.bit_length() - 1
    pitch = seg_len + LRU_SEGMENT_PAD
    piece = min(LRU_CHUNK, seg_len)
    halo = SUBLANES
    tile = pl.program_id(1)

    @pl.when((pl.program_id(0) == 0) & (tile == 0))
    def _init_tables():
        _init_attention_tables(dist_ref, slope_ref)

    def scan_row(t):
        return pl.multiple_of(t + (t >> seg_shift) * LRU_SEGMENT_PAD, LRU_SEGMENT_PAD)

    zeros = jnp.zeros((halo, LANES), F32)
    xpad_ref[0:halo, :] = zeros
    xpad_ref[seq + halo:seq + 2 * halo, :] = zeros

    def copy_body(c, carry):
        t0 = pl.multiple_of(c * LRU_CHUNK, LRU_CHUNK)
        xpad_ref[pl.ds(t0 + halo, LRU_CHUNK), :] = xr_ref[0, 0, pl.ds(t0, LRU_CHUNK), :]
        return carry

    lax.fori_loop(0, n_chunks, copy_body, 0)

    params = p_ref[0]
    conv_w = [params[P_CONV_W + j:P_CONV_W + j + 1] for j in range(CONV_WIDTH)]
    conv_b = params[P_CONV_B:P_CONV_B + 1]
    rates = [(0.5 * LRU_C * LOG2_E) * _log_sigmoid(params[P_LAM + d:P_LAM + d + 1]) for d in range(N_DIR)]
    a_refs = (af_ref, ab_ref)
    u_refs = (uf_ref, ub_ref)
    bias_cols = jnp.where(lax.broadcasted_iota(jnp.int32, (LRU_CHUNK, LANES), 1) < BIAS_SPLIT,
                          1.0, 0.0).astype(BF16)

    def recurrence_inputs(c):
        t0 = pl.multiple_of(c * LRU_CHUNK, LRU_CHUNK)
        xc = conv_b
        for j in range(CONV_WIDTH):
            xc = xc + xpad_ref[pl.ds(t0 + halo - CONV_PAD_LEFT + j, LRU_CHUNK), :] * conv_w[j]
        xc_aug = jnp.concatenate([xc.astype(BF16), bias_cols], axis=1)
        t = jnp.tanh(jnp.dot(xc_aug, wg_ref[0], preferred_element_type=F32))
        x_half = 0.5 * xc
        for d in range(N_DIR):
            t_r = t[:, 2 * LANES * d:2 * LANES * d + LANES]
            t_i = t[:, 2 * LANES * d + LANES:2 * LANES * (d + 1)]
            a = jnp.exp2(rates[d] + rates[d] * t_r)
            s = jnp.maximum(1.0 - a * a, 0.0)
            u = (s * lax.rsqrt(jnp.maximum(s, SQRT_FLOOR))) * (x_half + x_half * t_i)
            for lo in range(0, LRU_CHUNK, piece):
                row0 = scan_row(t0 + lo)
                a_refs[d][pl.ds(row0, piece), :] = a[lo:lo + piece]
                u_refs[d][pl.ds(row0, piece), :] = u[lo:lo + piece]

    def block_body(i, carry):
        r0 = pl.multiple_of(i * BLOCK, BLOCK)
        for j in range(LRU_CHUNKS_PER_BLOCK):
            recurrence_inputs(i * LRU_CHUNKS_PER_BLOCK + j)
        outs = _attention_block(tile * blocks_per_step + i, n_blocks, q_ref[0, pl.ds(r0, BLOCK), :],
                                sink_ref, k_ref, v_ref, dist_ref, slope_ref)
        for g, out in enumerate(outs):
            attn_ref[0, pl.ds(r0, BLOCK), g * LANES:(g + 1) * LANES] = out
        return carry

    lax.fori_loop(0, blocks_per_step, block_body, 0)

    def segment_rows(k, group):
        return pl.ds(group * SUBLANES * pitch + k, SUBLANES, stride=pitch)

    groups = range(LRU_SEGMENT_GROUPS)
    n_steps = seg_len // LRU_SCAN_UNROLL

    def summary_body(s, carry):
        prod_f, h_f, prod_b, h_b = [list(v) for v in carry]
        for i in range(LRU_SCAN_UNROLL):
            kf = s * LRU_SCAN_UNROLL + i
            kb = seg_len - 1 - kf
            for g in groups:
                a = af_ref[segment_rows(kf, g), :]
                h_f[g] = a * h_f[g] + uf_ref[segment_rows(kf, g), :]
                prod_f[g] = prod_f[g] * a
                a = ab_ref[segment_rows(kb, g), :]
                h_b[g] = a * h_b[g] + ub_ref[segment_rows(kb, g), :]
                prod_b[g] = prod_b[g] * a
        return tuple(prod_f), tuple(h_f), tuple(prod_b), tuple(h_b)

    ones = tuple(jnp.ones((SUBLANES, LANES), F32) for _ in groups)
    zeros_g = tuple(jnp.zeros((SUBLANES, LANES), F32) for _ in groups)
    prod_f, end_f, prod_b, end_b = lax.fori_loop(0, n_steps, summary_body, (ones, zeros_g, ones, zeros_g))
    prod_f, end_f, prod_b, end_b = [jnp.concatenate(v, axis=0) for v in (prod_f, end_f, prod_b, end_b)]

    entry = jnp.zeros((1, LANES), F32)
    entries_f = [entry]
    for j in range(1, LRU_SEGMENTS):
        entry = prod_f[j - 1:j] * entry + end_f[j - 1:j]
        entries_f.append(entry)
    entry = jnp.zeros((1, LANES), F32)
    entries_b = [entry]
    for j in range(LRU_SEGMENTS - 2, -1, -1):
        entry = prod_b[j + 1:j + 2] * entry + end_b[j + 1:j + 2]
        entries_b.append(entry)
    entries_b = entries_b[::-1]
    start_f = tuple(jnp.concatenate(entries_f[g * SUBLANES:(g + 1) * SUBLANES], axis=0) for g in groups)
    start_b = tuple(jnp.concatenate(entries_b[g * SUBLANES:(g + 1) * SUBLANES], axis=0) for g in groups)

    def scan_body(s, carry):
        h_f, h_b = [list(v) for v in carry]
        for i in range(LRU_SCAN_UNROLL):
            kf = s * LRU_SCAN_UNROLL + i
            kb = seg_len - 1 - kf
            for g in groups:
                h_f[g] = af_ref[segment_rows(kf, g), :] * h_f[g] + uf_ref[segment_rows(kf, g), :]
                hf_ref[segment_rows(kf, g), :] = h_f[g]
                h_b[g] = ab_ref[segment_rows(kb, g), :] * h_b[g] + ub_ref[segment_rows(kb, g), :]
                hb_ref[segment_rows(kb, g), :] = h_b[g]
        return tuple(h_f), tuple(h_b)

    lax.fori_loop(0, n_steps, scan_body, (start_f, start_b))

    def out_body(c, carry):
        t0 = pl.multiple_of(c * LRU_CHUNK, LRU_CHUNK)
        for lo in range(0, LRU_CHUNK, piece):
            row0 = scan_row(t0 + lo)
            h_ref[0, 0, pl.ds(t0 + lo, piece), :] = hf_ref[pl.ds(row0, piece), :] + hb_ref[pl.ds(row0, piece), :]
        return carry

    lax.fori_loop(0, n_chunks, out_body, 0)


def _attention_and_lru(q, k, v, sink, xr, w_gates, params):
    batch, seq, _ = q.shape
    rows_q = seq // LRU_TILES
    seg_len = seq // LRU_SEGMENTS
    assert seq // BLOCK >= 2 and rows_q % BLOCK == 0
    assert seq // LRU_CHUNK == (rows_q // BLOCK) * LRU_CHUNKS_PER_BLOCK
    assert seg_len & (seg_len - 1) == 0 and seg_len % LRU_SCAN_UNROLL == 0
    assert LRU_CHUNK % seg_len == 0 or seg_len % LRU_CHUNK == 0
    scan_rows = LRU_SEGMENTS * (seg_len + LRU_SEGMENT_PAD)
    slab = pl.BlockSpec((1, 1, seq, LANES), lambda b, c: (b, c, 0, 0))
    return pl.pallas_call(
        _attn_lru_kernel,
        out_shape=(
            jax.ShapeDtypeStruct((batch, seq, D_ATTN), F32),
            jax.ShapeDtypeStruct((batch, LRU_TILES, seq, LANES), F32),
        ),
        grid=(batch, LRU_TILES),
        in_specs=[
            pl.BlockSpec(memory_space=pltpu.SMEM),
            pl.BlockSpec((1, rows_q, D_ATTN), lambda b, c: (b, c, 0)),
            pl.BlockSpec((1, seq, KV_DIM), lambda b, c: (b, 0, 0)),
            pl.BlockSpec((1, seq, KV_DIM), lambda b, c: (b, 0, 0)),
            slab,
            pl.BlockSpec((1, 2 * LANES, 2 * N_DIR * LANES), lambda b, c: (c, 0, 0)),
            pl.BlockSpec((1, P_ROWS, LANES), lambda b, c: (c, 0, 0)),
        ],
        out_specs=(
            pl.BlockSpec((1, rows_q, D_ATTN), lambda b, c: (b, c, 0)),
            slab,
        ),
        scratch_shapes=[
            pltpu.VMEM((N_BIAS_VARIANTS, N_KEY_BLOCKS * BLOCK, LANES), BF16),
            pltpu.VMEM((N_Q_HEADS * BLOCK, LANES), BF16),
            pltpu.VMEM((seq + 2 * SUBLANES, LANES), F32),
        ] + [pltpu.VMEM((scan_rows, LANES), F32) for _ in range(3 * N_DIR)],
        compiler_params=_compiler_params(2),
        name="attention_lru",
    )(sink, q, k, v, xr, w_gates, params)


def _gelu_tanh(x):
    cdf = 0.5 * (1.0 + jnp.tanh(float(np.sqrt(2.0 / np.pi)) * (x + 0.044715 * (x * x * x))))
    return x * cdf


def _mix_ffn_kernel(x_ref, attn_ref, lru_ref, gate_ref, mod_ref, ga_ref, gl_ref, wo_ref, g2_ref, wfi_ref,
                    wfo_ref, gf_ref, o_ref, act_ref, *, final):
    x = x_ref[0]
    mod = mod_ref[0]
    attn = _rms(attn_ref[0]) * ga_ref[...]
    lru = jnp.concatenate([lru_ref[0, c] * _gelu_tanh(gate_ref[0, c]) for c in range(LRU_TILES)], axis=-1)
    lru = _rms(lru) * gl_ref[...]
    mix = jnp.concatenate([attn, lru], axis=-1).astype(BF16)
    x = x + mod[2:3] * jnp.dot(mix, wo_ref[...], preferred_element_type=F32)

    h = (_rms(x) * (g2_ref[...] * (1.0 + mod[4:5])) + mod[3:4]).astype(BF16)
    for j in range(D_FF // FF_COL_TILE):
        lo = j * FF_COL_TILE
        g = jnp.dot(h, wfi_ref[:, lo:lo + FF_COL_TILE], preferred_element_type=F32)
        u = jnp.dot(h, wfi_ref[:, D_FF + lo:D_FF + lo + FF_COL_TILE], preferred_element_type=F32)
        act_ref[:, lo:lo + FF_COL_TILE] = ((g * jax.nn.sigmoid(g)) * u).astype(BF16)
    x = x + mod[5:6] * jnp.dot(act_ref[...], wfo_ref[...], preferred_element_type=F32)
    if final:
        x = _rms(x) * gf_ref[...]
    o_ref[0] = x


def _mix_ffn(x, attn, lru, gate, mod, g_attn, g_lru, w_out, g_norm2, w_ffn_in, w_ffn_out, g_final, final):
    batch, seq, _ = x.shape
    tm = TOKEN_TILE
    tok = lambda b, i: (b, i, 0)
    const2 = lambda b, i: (0, 0)
    lru_spec = pl.BlockSpec((1, LRU_TILES, tm, LANES), lambda b, i: (b, 0, i, 0))
    resident = pl.Buffered(1)
    return pl.pallas_call(
        functools.partial(_mix_ffn_kernel, final=final),
        out_shape=jax.ShapeDtypeStruct((batch, seq, D_MODEL), F32),
        grid=(batch, seq // tm),
        in_specs=[
            pl.BlockSpec((1, tm, D_MODEL), tok),
            pl.BlockSpec((1, tm, D_ATTN), tok),
            lru_spec,
            lru_spec,
            pl.BlockSpec((1, N_MOD, D_MODEL), lambda b, i: (b, 0, 0)),
            pl.BlockSpec((1, D_ATTN), const2),
            pl.BlockSpec((1, D_LRU), const2),
            pl.BlockSpec((D_ATTN + D_LRU, D_MODEL), const2, pipeline_mode=resident),
            pl.BlockSpec((1, D_MODEL), const2),
            pl.BlockSpec((D_MODEL, 2 * D_FF), const2, pipeline_mode=resident),
            pl.BlockSpec((D_FF, D_MODEL), const2, pipeline_mode=resident),
            pl.BlockSpec((1, D_MODEL), const2),
        ],
        out_specs=pl.BlockSpec((1, tm, D_MODEL), tok),
        scratch_shapes=[pltpu.VMEM((tm, D_FF), BF16)],
        compiler_params=_compiler_params(2),
        name="mix_ffn",
    )(x, attn, lru, gate, mod, g_attn, g_lru, w_out, g_norm2, w_ffn_in, w_ffn_out, g_final)


def _paired_head_order():
    cols = []
    for g in range(Q_PER_KV):
        cols.append(np.arange(HEAD_DIM) + HEAD_DIM * g)
        cols.append(np.arange(HEAD_DIM) + HEAD_DIM * (Q_PER_KV + g))
    return np.concatenate(cols)


def _gate_weights(w_rg, b_rg, w_ig, b_ig):
    per_tile = LANES // LRU_BLOCK
    eye = jnp.eye(per_tile, dtype=F32)

    def dense(w):
        w = w.reshape(N_DIR, LRU_TILES, per_tile, LRU_BLOCK, LRU_BLOCK)
        full = w[:, :, :, :, None, :] * eye[None, None, :, None, :, None]
        return full.reshape(N_DIR, LRU_TILES, LANES, LANES)

    rg, ig = dense(w_rg), dense(w_ig)
    weights = (jnp.concatenate([rg[0], ig[0], rg[1], ig[1]], axis=-1) * 0.5).astype(BF16)

    def tiles(b):
        return b.reshape(N_DIR, LRU_TILES, LANES)

    brg, big = tiles(b_rg), tiles(b_ig)
    rest = jnp.concatenate([brg[0], big[0], brg[1], big[1]], axis=-1) * 0.5
    terms = []
    for _ in range(BIAS_SPLIT):
        term = rest.astype(BF16)
        terms.append(term)
        rest = rest - term.astype(F32)
    bias_rows = jnp.stack(terms, axis=1)
    pad = jnp.zeros((LRU_TILES, LANES - BIAS_SPLIT, 2 * N_DIR * LANES), BF16)
    return jnp.concatenate([weights, bias_rows, pad], axis=1)


def _lru_params(conv_w, conv_b, lam):
    table = jnp.concatenate([conv_w, conv_b[None], lam], axis=0)
    table = jnp.pad(table, ((0, P_ROWS - table.shape[0]), (0, 0)))
    return table.reshape(P_ROWS, LRU_TILES, LANES).transpose(1, 0, 2)


def kernel(x_prompt, x_sample, c_prompt, c_sample, w_mod, b_mod, g_norm1, w_in, sink, conv_w, conv_b,
           w_rg, b_rg, w_ig, b_ig, lam, g_attn_out, g_lru_out, w_out, g_norm2, w_ffn_in, w_ffn_out, g_final):
    order = _paired_head_order()
    n_prompt = c_prompt.shape[0]
    mod_all = _modulation(jnp.concatenate([c_prompt, c_sample], axis=0), w_mod, b_mod)
    mod_all = mod_all.reshape(DEPTH, -1, N_MOD, D_MODEL)

    w_in_b = jnp.concatenate([w_in[:, :, :D_ATTN][:, :, order], w_in[:, :, D_ATTN:]], axis=-1).astype(BF16)
    w_out_b = jnp.concatenate([w_out[:, :D_ATTN][:, order], w_out[:, D_ATTN:]], axis=1).astype(BF16)
    g_attn = g_attn_out[:, order]
    w_ffn_in_b = w_ffn_in.astype(BF16)
    w_ffn_out_b = w_ffn_out.astype(BF16)

    xs = [x_prompt, x_sample]
    for l in range(DEPTH):
        w_gates = _gate_weights(w_rg[l], b_rg[l], w_ig[l], b_ig[l])
        params = _lru_params(conv_w[l], conv_b[l], lam[l])
        mods = [mod_all[l, :n_prompt], mod_all[l, n_prompt:]]
        for t in range(2):
            x = xs[t]
            q, k, v, xr, gate = _in_projection(x, mods[t], g_norm1[l][None], w_in_b[l])
            attn, lru = _attention_and_lru(q, k, v, sink[l], xr, w_gates, params)
            xs[t] = _mix_ffn(x, attn, lru, gate, mods[t], g_attn[l][None], g_lru_out[l][None], w_out_b[l],
                             g_norm2[l][None], w_ffn_in_b[l], w_ffn_out_b[l], g_final[None],
                             final=(l == DEPTH - 1))
    return (xs[0], xs[1])
```

```python
import functools

import jax
import jax.numpy as jnp
import numpy as np
from jax import lax
from jax.experimental import pallas as pl
from jax.experimental.pallas import tpu as pltpu

F32 = jnp.float32
BF16 = jnp.bfloat16

D_MODEL = 1024
DEPTH = 4
D_ATTN = 512
N_Q_HEADS = 8
N_KV_HEADS = 2
HEAD_DIM = 64
Q_PER_KV = N_Q_HEADS // N_KV_HEADS
KV_DIM = N_KV_HEADS * HEAD_DIM
WINDOW = 128
BLOCK = 128
D_LRU = 512
N_LRU_BLOCKS = 8
LRU_BLOCK = D_LRU // N_LRU_BLOCKS
CONV_WIDTH = 4
CONV_PAD_LEFT = 2
LRU_C = 8.0
N_DIR = 2
D_IN = D_ATTN + 2 * KV_DIM + 2 * D_LRU
D_FF = 2816
N_MOD = 6
EPS = 1e-6
LOG2_E = float(np.log2(np.e))

LANES = 128
SUBLANES = 8
VMEM_LIMIT_BYTES = 56 * 1024 * 1024

TOKEN_TILE = 512
INPROJ_TILE = 1024
MOD_COL_TILE = 1536
FF_COL_TILE = 256
LRU_TILES = D_LRU // LANES
LRU_CHUNK = 256
LRU_SEGMENT_GROUPS = 4
LRU_SEGMENTS = SUBLANES * LRU_SEGMENT_GROUPS
LRU_SEGMENT_PAD = 4
LRU_SCAN_UNROLL = 4
SQRT_FLOOR = 1e-30
BIAS_SPLIT = 3
N_KEY_BLOCKS = 3
N_BIAS_VARIANTS = 3
MASKED_DISTANCE = -(2.0 ** 100)

P_CONV_W = 0
P_CONV_B = 4
P_LAM = 5
P_ROWS = 8


def _rms(x):
    return x * lax.rsqrt(jnp.mean(x * x, axis=-1, keepdims=True) + EPS)


def _compiler_params(n_grid_dims):
    return pltpu.CompilerParams(
        dimension_semantics=("arbitrary",) * n_grid_dims,
        vmem_limit_bytes=VMEM_LIMIT_BYTES,
    )


def _mod_kernel(c_ref, w_ref, b_ref, o_ref):
    c = c_ref[...]
    c_act = c * jax.nn.sigmoid(c)
    o_ref[0] = jnp.dot(c_act, w_ref[0], preferred_element_type=F32) + b_ref[0]


def _modulation(c_all, w_mod, b_mod):
    n_rows = c_all.shape[0]
    n_cols = N_MOD * D_MODEL
    return pl.pallas_call(
        _mod_kernel,
        out_shape=jax.ShapeDtypeStruct((DEPTH, n_rows, n_cols), F32),
        grid=(DEPTH, n_cols // MOD_COL_TILE),
        in_specs=[
            pl.BlockSpec((n_rows, D_MODEL), lambda l, j: (0, 0)),
            pl.BlockSpec((1, D_MODEL, MOD_COL_TILE), lambda l, j: (l, 0, j)),
            pl.BlockSpec((1, 1, MOD_COL_TILE), lambda l, j: (l, 0, j)),
        ],
        out_specs=pl.BlockSpec((1, n_rows, MOD_COL_TILE), lambda l, j: (l, 0, j)),
        compiler_params=_compiler_params(2),
        name="modulation",
    )(c_all, w_mod, b_mod.reshape(DEPTH, 1, n_cols))


def _inproj_kernel(x_ref, mod_ref, g_ref, w_ref, q_ref, k_ref, v_ref, xr_ref, gate_ref):
    x = x_ref[0]
    mod = mod_ref[0]
    h = _rms(x) * (g_ref[...] * (1.0 + mod[1:2])) + mod[0:1]
    z = jnp.dot(h.astype(BF16), w_ref[...], preferred_element_type=F32)
    q_ref[0] = (z[:, :D_ATTN] * (HEAD_DIM ** -0.5)).astype(BF16)
    k_ref[0] = z[:, D_ATTN:D_ATTN + KV_DIM].astype(BF16)
    v_ref[0] = z[:, D_ATTN + KV_DIM:D_ATTN + 2 * KV_DIM].astype(BF16)
    base = D_ATTN + 2 * KV_DIM
    for c in range(LRU_TILES):
        xr_ref[0, c] = z[:, base + c * LANES:base + (c + 1) * LANES]
        gate_ref[0, c] = z[:, base + D_LRU + c * LANES:base + D_LRU + (c + 1) * LANES]


def _in_projection(x, mod, g_norm1, w_in):
    batch, seq, _ = x.shape
    tm = INPROJ_TILE
    tok = lambda b, i: (b, i, 0)
    lru_spec = pl.BlockSpec((1, LRU_TILES, tm, LANES), lambda b, i: (b, 0, i, 0))
    return pl.pallas_call(
        _inproj_kernel,
        out_shape=(
            jax.ShapeDtypeStruct((batch, seq, D_ATTN), BF16),
            jax.ShapeDtypeStruct((batch, seq, KV_DIM), BF16),
            jax.ShapeDtypeStruct((batch, seq, KV_DIM), BF16),
            jax.ShapeDtypeStruct((batch, LRU_TILES, seq, LANES), F32),
            jax.ShapeDtypeStruct((batch, LRU_TILES, seq, LANES), F32),
        ),
        grid=(batch, seq // tm),
        in_specs=[
            pl.BlockSpec((1, tm, D_MODEL), tok),
            pl.BlockSpec((1, N_MOD, D_MODEL), lambda b, i: (b, 0, 0)),
            pl.BlockSpec((1, D_MODEL), lambda b, i: (0, 0)),
            pl.BlockSpec((D_MODEL, D_IN), lambda b, i: (0, 0)),
        ],
        out_specs=(
            pl.BlockSpec((1, tm, D_ATTN), tok),
            pl.BlockSpec((1, tm, KV_DIM), tok),
            pl.BlockSpec((1, tm, KV_DIM), tok),
            lru_spec,
            lru_spec,
        ),
        compiler_params=_compiler_params(2),
        name="in_projection",
    )(x, mod, g_norm1, w_in)


def _slab_head(j):
    return j // 2 + Q_PER_KV * (j % 2)


def _init_attention_tables(dist_ref, slope_ref):
    n_keys = N_KEY_BLOCKS * BLOCK
    kc = lax.broadcasted_iota(jnp.int32, (n_keys, BLOCK), 0)
    qi = lax.broadcasted_iota(jnp.int32, (n_keys, BLOCK), 1)
    dist = jnp.abs(BLOCK + qi - kc)
    neg_dist = -dist.astype(F32)
    inside = dist <= WINDOW
    for variant in range(N_BIAS_VARIANTS):
        ok = inside
        if variant == 1:
            ok = ok & (kc >= BLOCK)
        if variant == 2:
            ok = ok & (kc < 2 * BLOCK)
        dist_ref[variant] = jnp.where(ok, neg_dist, MASKED_DISTANCE).astype(BF16)
    row = lax.broadcasted_iota(jnp.int32, (BLOCK, LANES), 0)
    col = lax.broadcasted_iota(jnp.int32, (BLOCK, LANES), 1)
    for j in range(N_Q_HEADS):
        slope = 2.0 ** -(_slab_head(j) + 1)
        slope_ref[j * BLOCK:(j + 1) * BLOCK, :] = jnp.where(row == col, slope, 0.0).astype(BF16)


def _attention_block(n, n_blocks, q, sink_ref, k_ref, v_ref, dist_ref, slope_ref):
    low = lax.broadcasted_iota(jnp.int32, (BLOCK, LANES), 1) < HEAD_DIM
    zero = jnp.zeros((BLOCK, LANES), BF16)
    ones = jnp.ones((N_KEY_BLOCKS * BLOCK, LANES), BF16)
    prev_start = pl.multiple_of(jnp.maximum(n - 1, 0) * BLOCK, BLOCK)
    cur_start = pl.multiple_of(n * BLOCK, BLOCK)
    next_start = pl.multiple_of(jnp.minimum(n + 1, n_blocks - 1) * BLOCK, BLOCK)
    starts = (prev_start, cur_start, next_start)
    k3 = jnp.concatenate([k_ref[0, pl.ds(s, BLOCK), :] for s in starts], axis=0)
    v3 = jnp.concatenate([v_ref[0, pl.ds(s, BLOCK), :] for s in starts], axis=0)
    variant = jnp.where(n == 0, 1, jnp.where(n == n_blocks - 1, 2, 0))
    k_aug = jnp.concatenate([k3, dist_ref[variant]], axis=1)
    v_aug = jnp.concatenate([v3, ones], axis=1)

    slabs = []
    for g in range(D_ATTN // LANES):
        qg = q[:, g * LANES:(g + 1) * LANES]
        slabs.append(jnp.where(low, qg, zero))
        slabs.append(jnp.where(low, zero, qg))
    q_aug = jnp.concatenate([jnp.concatenate(slabs, axis=0), slope_ref[...]], axis=1)
    logits_all = lax.dot_general(q_aug, k_aug, (((1,), (1,)), ((), ())),
                                 preferred_element_type=F32)
    outs = []
    for j in range(N_Q_HEADS):
        sink = sink_ref[_slab_head(j)]
        logits = logits_all[j * BLOCK:(j + 1) * BLOCK]
        m = jnp.max(logits, axis=-1, keepdims=True)
        p = jnp.exp(logits - m).astype(BF16)
        pv = jnp.dot(p, v_aug, preferred_element_type=F32)
        denom = pv[:, LANES:] + jnp.exp(sink - m)
        outs.append(pv[:, :LANES] * (1.0 / denom))
    return [jnp.where(low, outs[2 * g], outs[2 * g + 1]) for g in range(D_ATTN // LANES)]


def _log_sigmoid(x):
    return jnp.minimum(x, 0.0) - jnp.log1p(jnp.exp(-jnp.abs(x)))


def _attn_lru_kernel(sink_ref, q_ref, k_ref, v_ref, xr_ref, wg_ref, p_ref, attn_ref, h_ref,
                     dist_ref, slope_ref, xpad_ref, af_ref, uf_ref, ab_ref, ub_ref, hf_ref, hb_ref):
    seq = xr_ref.shape[2]
    n_blocks = seq // BLOCK
    blocks_per_step = q_ref.shape[1] // BLOCK
    n_chunks = seq // LRU_CHUNK
    seg_len = seq // LRU_SEGMENTS
    seg_shift = seg_len.bit_length() - 1
    pitch = seg_len + LRU_SEGMENT_PAD
    piece = min(LRU_CHUNK, seg_len)
    halo = SUBLANES
    tile = pl.program_id(1)

    @pl.when((pl.program_id(0) == 0) & (tile == 0))
    def _init_tables():
        _init_attention_tables(dist_ref, slope_ref)

    def scan_row(t):
        return pl.multiple_of(t + (t >> seg_shift) * LRU_SEGMENT_PAD, LRU_SEGMENT_PAD)

    zeros = jnp.zeros((halo, LANES), F32)
    xpad_ref[0:halo, :] = zeros
    xpad_ref[seq + halo:seq + 2 * halo, :] = zeros

    def copy_body(c, carry):
        t0 = pl.multiple_of(c * LRU_CHUNK, LRU_CHUNK)
        xpad_ref[pl.ds(t0 + halo, LRU_CHUNK), :] = xr_ref[0, 0, pl.ds(t0, LRU_CHUNK), :]
        return carry

    lax.fori_loop(0, n_chunks, copy_body, 0)

    params = p_ref[0]
    conv_w = [params[P_CONV_W + j:P_CONV_W + j + 1] for j in range(CONV_WIDTH)]
    conv_b = params[P_CONV_B:P_CONV_B + 1]
    rates = [(0.5 * LRU_C * LOG2_E) * _log_sigmoid(params[P_LAM + d:P_LAM + d + 1]) for d in range(N_DIR)]
    a_refs = (af_ref, ab_ref)
    u_refs = (uf_ref, ub_ref)
    bias_cols = jnp.where(lax.broadcasted_iota(jnp.int32, (LRU_CHUNK, LANES), 1) < BIAS_SPLIT,
                          1.0, 0.0).astype(BF16)

    def recurrence_inputs(c):
        t0 = pl.multiple_of(c * LRU_CHUNK, LRU_CHUNK)
        x_half = conv_b
        for j in range(CONV_WIDTH):
            x_half = x_half + xpad_ref[pl.ds(t0 + halo - CONV_PAD_LEFT + j, LRU_CHUNK), :] * conv_w[j]
        x_aug = jnp.concatenate([x_half.astype(BF16), bias_cols], axis=1)
        t = jnp.tanh(jnp.dot(x_aug, wg_ref[0], preferred_element_type=F32))
        for d in range(N_DIR):
            t_r = t[:, 2 * LANES * d:2 * LANES * d + LANES]
            t_i = t[:, 2 * LANES * d + LANES:2 * LANES * (d + 1)]
            a = jnp.exp2(rates[d] + rates[d] * t_r)
            s = jnp.maximum(1.0 - a * a, 0.0)
            u = (s * lax.rsqrt(jnp.maximum(s, SQRT_FLOOR))) * (x_half + x_half * t_i)
            for lo in range(0, LRU_CHUNK, piece):
                row0 = scan_row(t0 + lo)
                a_refs[d][pl.ds(row0, piece), :] = a[lo:lo + piece]
                u_refs[d][pl.ds(row0, piece), :] = u[lo:lo + piece]

    def attention_block(i):
        r0 = pl.multiple_of(i * BLOCK, BLOCK)
        outs = _attention_block(tile * blocks_per_step + i, n_blocks, q_ref[0, pl.ds(r0, BLOCK), :],
                                sink_ref, k_ref, v_ref, dist_ref, slope_ref)
        for g, out in enumerate(outs):
            attn_ref[0, pl.ds(r0, BLOCK), g * LANES:(g + 1) * LANES] = out

    chunks_per_block = n_chunks // blocks_per_step

    def dense_body(i, carry):
        for j in range(chunks_per_block):
            recurrence_inputs(i * chunks_per_block + j)
        attention_block(i)
        return carry

    lax.fori_loop(0, blocks_per_step, dense_body, 0, unroll=4)

    def segment_rows(k, group):
        return pl.ds(group * SUBLANES * pitch + k, SUBLANES, stride=pitch)

    groups = range(LRU_SEGMENT_GROUPS)
    n_steps = seg_len // LRU_SCAN_UNROLL

    def summary_body(s, carry):
        prod_f, h_f, prod_b, h_b = [list(v) for v in carry]
        for i in range(LRU_SCAN_UNROLL):
            kf = s * LRU_SCAN_UNROLL + i
            kb = seg_len - 1 - kf
            for g in groups:
                a = af_ref[segment_rows(kf, g), :]
                h_f[g] = a * h_f[g] + uf_ref[segment_rows(kf, g), :]
                prod_f[g] = prod_f[g] * a
                a = ab_ref[segment_rows(kb, g), :]
                h_b[g] = a * h_b[g] + ub_ref[segment_rows(kb, g), :]
                prod_b[g] = prod_b[g] * a
        return tuple(prod_f), tuple(h_f), tuple(prod_b), tuple(h_b)

    ones = tuple(jnp.ones((SUBLANES, LANES), F32) for _ in groups)
    zeros_g = tuple(jnp.zeros((SUBLANES, LANES), F32) for _ in groups)
    prod_f, end_f, prod_b, end_b = lax.fori_loop(0, n_steps, summary_body, (ones, zeros_g, ones, zeros_g))
    prod_f, end_f, prod_b, end_b = [jnp.concatenate(v, axis=0) for v in (prod_f, end_f, prod_b, end_b)]

    entry = jnp.zeros((1, LANES), F32)
    entries_f = [entry]
    for j in range(1, LRU_SEGMENTS):
        entry = prod_f[j - 1:j] * entry + end_f[j - 1:j]
        entries_f.append(entry)
    entry = jnp.zeros((1, LANES), F32)
    entries_b = [entry]
    for j in range(LRU_SEGMENTS - 2, -1, -1):
        entry = prod_b[j + 1:j + 2] * entry + end_b[j + 1:j + 2]
        entries_b.append(entry)
    entries_b = entries_b[::-1]
    start_f = tuple(jnp.concatenate(entries_f[g * SUBLANES:(g + 1) * SUBLANES], axis=0) for g in groups)
    start_b = tuple(jnp.concatenate(entries_b[g * SUBLANES:(g + 1) * SUBLANES], axis=0) for g in groups)

    def scan_body(s, carry):
        h_f, h_b = [list(v) for v in carry]
        for i in range(LRU_SCAN_UNROLL):
            kf = s * LRU_SCAN_UNROLL + i
            kb = seg_len - 1 - kf
            for g in groups:
                h_f[g] = af_ref[segment_rows(kf, g), :] * h_f[g] + uf_ref[segment_rows(kf, g), :]
                hf_ref[segment_rows(kf, g), :] = h_f[g]
                h_b[g] = ab_ref[segment_rows(kb, g), :] * h_b[g] + ub_ref[segment_rows(kb, g), :]
                hb_ref[segment_rows(kb, g), :] = h_b[g]
        return tuple(h_f), tuple(h_b)

    lax.fori_loop(0, n_steps, scan_body, (start_f, start_b))

    def out_body(c, carry):
        t0 = pl.multiple_of(c * LRU_CHUNK, LRU_CHUNK)
        for lo in range(0, LRU_CHUNK, piece):
            row0 = scan_row(t0 + lo)
            h_ref[0, 0, pl.ds(t0 + lo, piece), :] = hf_ref[pl.ds(row0, piece), :] + hb_ref[pl.ds(row0, piece), :]
        return carry

    lax.fori_loop(0, n_chunks, out_body, 0)


def _attention_and_lru(q, k, v, sink, xr, w_gates, params):
    batch, seq, _ = q.shape
    rows_q = seq // LRU_TILES
    seg_len = seq // LRU_SEGMENTS
    assert seq // BLOCK >= 2 and rows_q % BLOCK == 0
    assert (seq // LRU_CHUNK) % (rows_q // BLOCK) == 0
    assert seg_len & (seg_len - 1) == 0 and seg_len % LRU_SCAN_UNROLL == 0
    assert LRU_CHUNK % seg_len == 0 or seg_len % LRU_CHUNK == 0
    scan_rows = LRU_SEGMENTS * (seg_len + LRU_SEGMENT_PAD)
    slab = pl.BlockSpec((1, 1, seq, LANES), lambda b, c: (b, c, 0, 0))
    return pl.pallas_call(
        _attn_lru_kernel,
        out_shape=(
            jax.ShapeDtypeStruct((batch, seq, D_ATTN), F32),
            jax.ShapeDtypeStruct((batch, LRU_TILES, seq, LANES), F32),
        ),
        grid=(batch, LRU_TILES),
        in_specs=[
            pl.BlockSpec(memory_space=pltpu.SMEM),
            pl.BlockSpec((1, rows_q, D_ATTN), lambda b, c: (b, c, 0)),
            pl.BlockSpec((1, seq, KV_DIM), lambda b, c: (b, 0, 0)),
            pl.BlockSpec((1, seq, KV_DIM), lambda b, c: (b, 0, 0)),
            slab,
            pl.BlockSpec((1, 2 * LANES, 2 * N_DIR * LANES), lambda b, c: (c, 0, 0)),
            pl.BlockSpec((1, P_ROWS, LANES), lambda b, c: (c, 0, 0)),
        ],
        out_specs=(
            pl.BlockSpec((1, rows_q, D_ATTN), lambda b, c: (b, c, 0)),
            slab,
        ),
        scratch_shapes=[
            pltpu.VMEM((N_BIAS_VARIANTS, N_KEY_BLOCKS * BLOCK, LANES), BF16),
            pltpu.VMEM((N_Q_HEADS * BLOCK, LANES), BF16),
            pltpu.VMEM((seq + 2 * SUBLANES, LANES), F32),
        ] + [pltpu.VMEM((scan_rows, LANES), F32) for _ in range(3 * N_DIR)],
        compiler_params=_compiler_params(2),
        name="attention_lru",
    )(sink, q, k, v, xr, w_gates, params)


def _gelu_tanh(x):
    cdf = 0.5 * (1.0 + jnp.tanh(float(np.sqrt(2.0 / np.pi)) * (x + 0.044715 * (x * x * x))))
    return x * cdf


def _mix_ffn_kernel(x_ref, attn_ref, lru_ref, gate_ref, mod_ref, ga_ref, gl_ref, wo_ref, g2_ref, wfi_ref,
                    wfo_ref, gf_ref, o_ref, act_ref, *, final):
    x = x_ref[0]
    mod = mod_ref[0]
    attn = _rms(attn_ref[0]) * ga_ref[...]
    lru = jnp.concatenate([lru_ref[0, c] * _gelu_tanh(gate_ref[0, c]) for c in range(LRU_TILES)], axis=-1)
    lru = _rms(lru) * gl_ref[...]
    mix = jnp.concatenate([attn, lru], axis=-1).astype(BF16)
    x = x + mod[2:3] * jnp.dot(mix, wo_ref[...], preferred_element_type=F32)

    h = (_rms(x) * (g2_ref[...] * (1.0 + mod[4:5])) + mod[3:4]).astype(BF16)
    for j in range(D_FF // FF_COL_TILE):
        lo = j * FF_COL_TILE
        g = jnp.dot(h, wfi_ref[:, lo:lo + FF_COL_TILE], preferred_element_type=F32)
        u = jnp.dot(h, wfi_ref[:, D_FF + lo:D_FF + lo + FF_COL_TILE], preferred_element_type=F32)
        act_ref[:, lo:lo + FF_COL_TILE] = ((g * jax.nn.sigmoid(g)) * u).astype(BF16)
    x = x + mod[5:6] * jnp.dot(act_ref[...], wfo_ref[...], preferred_element_type=F32)
    if final:
        x = _rms(x) * gf_ref[...]
    o_ref[0] = x


def _mix_ffn(x, attn, lru, gate, mod, g_attn, g_lru, w_out, g_norm2, w_ffn_in, w_ffn_out, g_final, final):
    batch, seq, _ = x.shape
    tm = TOKEN_TILE
    tok = lambda b, i: (b, i, 0)
    const2 = lambda b, i: (0, 0)
    lru_spec = pl.BlockSpec((1, LRU_TILES, tm, LANES), lambda b, i: (b, 0, i, 0))
    resident = pl.Buffered(1)
    return pl.pallas_call(
        functools.partial(_mix_ffn_kernel, final=final),
        out_shape=jax.ShapeDtypeStruct((batch, seq, D_MODEL), F32),
        grid=(batch, seq // tm),
        in_specs=[
            pl.BlockSpec((1, tm, D_MODEL), tok),
            pl.BlockSpec((1, tm, D_ATTN), tok),
            lru_spec,
            lru_spec,
            pl.BlockSpec((1, N_MOD, D_MODEL), lambda b, i: (b, 0, 0)),
            pl.BlockSpec((1, D_ATTN), const2),
            pl.BlockSpec((1, D_LRU), const2),
            pl.BlockSpec((D_ATTN + D_LRU, D_MODEL), const2, pipeline_mode=resident),
            pl.BlockSpec((1, D_MODEL), const2),
            pl.BlockSpec((D_MODEL, 2 * D_FF), const2, pipeline_mode=resident),
            pl.BlockSpec((D_FF, D_MODEL), const2, pipeline_mode=resident),
            pl.BlockSpec((1, D_MODEL), const2),
        ],
        out_specs=pl.BlockSpec((1, tm, D_MODEL), tok),
        scratch_shapes=[pltpu.VMEM((tm, D_FF), BF16)],
        compiler_params=_compiler_params(2),
        name="mix_ffn",
    )(x, attn, lru, gate, mod, g_attn, g_lru, w_out, g_norm2, w_ffn_in, w_ffn_out, g_final)


def _pair_heads(a, axis):
    shape = a.shape
    a = a.reshape(shape[:axis] + (N_KV_HEADS, Q_PER_KV, HEAD_DIM) + shape[axis + 1:])
    return jnp.swapaxes(a, axis, axis + 1).reshape(shape)


def _gate_weights(w_rg, b_rg, w_ig, b_ig):
    per_tile = LANES // LRU_BLOCK
    eye = jnp.eye(per_tile, dtype=F32)

    def dense(w):
        w = w.reshape(N_DIR, LRU_TILES, per_tile, LRU_BLOCK, LRU_BLOCK)
        full = w[:, :, :, :, None, :] * eye[None, None, :, None, :, None]
        return full.reshape(N_DIR, LRU_TILES, LANES, LANES)

    rg, ig = dense(w_rg), dense(w_ig)
    weights = jnp.concatenate([rg[0], ig[0], rg[1], ig[1]], axis=-1).astype(BF16)

    def tiles(b):
        return b.reshape(N_DIR, LRU_TILES, LANES)

    brg, big = tiles(b_rg), tiles(b_ig)
    rest = jnp.concatenate([brg[0], big[0], brg[1], big[1]], axis=-1) * 0.5
    terms = []
    for _ in range(BIAS_SPLIT):
        term = rest.astype(BF16)
        terms.append(term)
        rest = rest - term.astype(F32)
    bias_rows = jnp.stack(terms, axis=1)
    pad = jnp.zeros((LRU_TILES, LANES - BIAS_SPLIT, 2 * N_DIR * LANES), BF16)
    return jnp.concatenate([weights, bias_rows, pad], axis=1)


def _lru_params(conv_w, conv_b, lam):
    table = jnp.concatenate([0.5 * conv_w, 0.5 * conv_b[None], lam], axis=0)
    table = jnp.pad(table, ((0, P_ROWS - table.shape[0]), (0, 0)))
    return table.reshape(P_ROWS, LRU_TILES, LANES).transpose(1, 0, 2)


def kernel(x_prompt, x_sample, c_prompt, c_sample, w_mod, b_mod, g_norm1, w_in, sink, conv_w, conv_b,
           w_rg, b_rg, w_ig, b_ig, lam, g_attn_out, g_lru_out, w_out, g_norm2, w_ffn_in, w_ffn_out, g_final):
    n_prompt = c_prompt.shape[0]
    mod_all = _modulation(jnp.concatenate([c_prompt, c_sample], axis=0), w_mod, b_mod)
    mod_all = mod_all.reshape(DEPTH, -1, N_MOD, D_MODEL)

    w_in_b = w_in.astype(BF16)
    w_in_b = jnp.concatenate([_pair_heads(w_in_b[:, :, :D_ATTN], 2), w_in_b[:, :, D_ATTN:]], axis=-1)
    w_out_b = w_out.astype(BF16)
    w_out_b = jnp.concatenate([_pair_heads(w_out_b[:, :D_ATTN], 1), w_out_b[:, D_ATTN:]], axis=1)
    g_attn = _pair_heads(g_attn_out, 1)
    w_ffn_in_b = w_ffn_in.astype(BF16)
    w_ffn_out_b = w_ffn_out.astype(BF16)

    xs = [x_prompt, x_sample]
    for l in range(DEPTH):
        w_gates = _gate_weights(w_rg[l], b_rg[l], w_ig[l], b_ig[l])
        params = _lru_params(conv_w[l], conv_b[l], lam[l])
        mods = [mod_all[l, :n_prompt], mod_all[l, n_prompt:]]
        for t in range(2):
            x = xs[t]
            q, k, v, xr, gate = _in_projection(x, mods[t], g_norm1[l][None], w_in_b[l])
            attn, lru = _attention_and_lru(q, k, v, sink[l], xr, w_gates, params)
            xs[t] = _mix_ffn(x, attn, lru, gate, mods[t], g_attn[l][None], g_lru_out[l][None], w_out_b[l],
                             g_norm2[l][None], w_ffn_in_b[l], w_ffn_out_b[l], g_final[None],
                             final=(l == DEPTH - 1))
    return (xs[0], xs[1])
```

```python
import functools

import jax
import jax.numpy as jnp
import numpy as np
from jax import lax
from jax.experimental import pallas as pl
from jax.experimental.pallas import tpu as pltpu

F32 = jnp.float32
BF16 = jnp.bfloat16

D_MODEL = 1024
DEPTH = 4
D_ATTN = 512
N_Q_HEADS = 8
N_KV_HEADS = 2
HEAD_DIM = 64
Q_PER_KV = N_Q_HEADS // N_KV_HEADS
KV_DIM = N_KV_HEADS * HEAD_DIM
WINDOW = 128
BLOCK = 128
D_LRU = 512
N_LRU_BLOCKS = 8
LRU_BLOCK = D_LRU // N_LRU_BLOCKS
CONV_WIDTH = 4
CONV_PAD_LEFT = 2
LRU_C = 8.0
N_DIR = 2
D_IN = D_ATTN + 2 * KV_DIM + 2 * D_LRU
D_FF = 2816
N_MOD = 6
EPS = 1e-6
LOG2_E = float(np.log2(np.e))

LANES = 128
SUBLANES = 8
VMEM_LIMIT_BYTES = 56 * 1024 * 1024

TOKEN_TILE = 512
INPROJ_TILE = 1024
MOD_COL_TILE = 1536
FF_COL_TILE = 256
LRU_TILES = D_LRU // LANES
LRU_CHUNK = 256
LRU_SEGMENT_GROUPS = 4
LRU_SEGMENTS = SUBLANES * LRU_SEGMENT_GROUPS
LRU_SEGMENT_PAD = 4
LRU_SCAN_UNROLL = 16
SQRT_FLOOR = 1e-30
BIAS_SPLIT = 3
N_KEY_BLOCKS = 3
N_BIAS_VARIANTS = 3
MASKED_DISTANCE = -(2.0 ** 100)

P_CONV_W = 0
P_CONV_B = 4
P_LAM = 5
P_ROWS = 8


def _rms(x):
    return x * lax.rsqrt(jnp.mean(x * x, axis=-1, keepdims=True) + EPS)


def _compiler_params(n_grid_dims):
    return pltpu.CompilerParams(
        dimension_semantics=("arbitrary",) * n_grid_dims,
        vmem_limit_bytes=VMEM_LIMIT_BYTES,
    )


def _mod_kernel(c_ref, w_ref, b_ref, o_ref):
    c = c_ref[...]
    c_act = c * jax.nn.sigmoid(c)
    o_ref[0] = jnp.dot(c_act, w_ref[0], preferred_element_type=F32) + b_ref[0]


def _modulation(c_all, w_mod, b_mod):
    n_rows = c_all.shape[0]
    n_cols = N_MOD * D_MODEL
    return pl.pallas_call(
        _mod_kernel,
        out_shape=jax.ShapeDtypeStruct((DEPTH, n_rows, n_cols), F32),
        grid=(DEPTH, n_cols // MOD_COL_TILE),
        in_specs=[
            pl.BlockSpec((n_rows, D_MODEL), lambda l, j: (0, 0)),
            pl.BlockSpec((1, D_MODEL, MOD_COL_TILE), lambda l, j: (l, 0, j)),
            pl.BlockSpec((1, 1, MOD_COL_TILE), lambda l, j: (l, 0, j)),
        ],
        out_specs=pl.BlockSpec((1, n_rows, MOD_COL_TILE), lambda l, j: (l, 0, j)),
        compiler_params=_compiler_params(2),
        name="modulation",
    )(c_all, w_mod, b_mod.reshape(DEPTH, 1, n_cols))


def _inproj_kernel(x_ref, mod_ref, g_ref, w_ref, q_ref, k_ref, v_ref, xr_ref, gate_ref):
    x = x_ref[0]
    mod = mod_ref[0]
    h = _rms(x) * (g_ref[...] * (1.0 + mod[1:2])) + mod[0:1]
    z = jnp.dot(h.astype(BF16), w_ref[...], preferred_element_type=F32)
    q_ref[0] = (z[:, :D_ATTN] * (HEAD_DIM ** -0.5)).astype(BF16)
    k_ref[0] = z[:, D_ATTN:D_ATTN + KV_DIM].astype(BF16)
    v_ref[0] = z[:, D_ATTN + KV_DIM:D_ATTN + 2 * KV_DIM].astype(BF16)
    base = D_ATTN + 2 * KV_DIM
    for c in range(LRU_TILES):
        xr_ref[0, c] = z[:, base + c * LANES:base + (c + 1) * LANES]
        gate_ref[0, c] = z[:, base + D_LRU + c * LANES:base + D_LRU + (c + 1) * LANES]


def _in_projection(x, mod, g_norm1, w_in, layer):
    batch, seq, _ = x.shape
    tm = INPROJ_TILE
    tok = lambda b, i: (b, i, 0)
    lru_spec = pl.BlockSpec((1, LRU_TILES, tm, LANES), lambda b, i: (b, 0, i, 0))
    return pl.pallas_call(
        _inproj_kernel,
        out_shape=(
            jax.ShapeDtypeStruct((batch, seq, D_ATTN), BF16),
            jax.ShapeDtypeStruct((batch, seq, KV_DIM), BF16),
            jax.ShapeDtypeStruct((batch, seq, KV_DIM), BF16),
            jax.ShapeDtypeStruct((batch, LRU_TILES, seq, LANES), F32),
            jax.ShapeDtypeStruct((batch, LRU_TILES, seq, LANES), F32),
        ),
        grid=(batch, seq // tm),
        in_specs=[
            pl.BlockSpec((1, tm, D_MODEL), tok),
            pl.BlockSpec((1, N_MOD, D_MODEL), lambda b, i: (b, 0, 0)),
            pl.BlockSpec((1, D_MODEL), lambda b, i: (0, 0)),
            pl.BlockSpec((None, D_MODEL, D_IN), lambda b, i: (layer, 0, 0)),
        ],
        out_specs=(
            pl.BlockSpec((1, tm, D_ATTN), tok),
            pl.BlockSpec((1, tm, KV_DIM), tok),
            pl.BlockSpec((1, tm, KV_DIM), tok),
            lru_spec,
            lru_spec,
        ),
        compiler_params=_compiler_params(2),
        name="in_projection",
    )(x, mod, g_norm1, w_in)


def _slab_head(j):
    return j // 2 + Q_PER_KV * (j % 2)


def _init_attention_tables(dist_ref, slope_ref):
    n_keys = N_KEY_BLOCKS * BLOCK
    kc = lax.broadcasted_iota(jnp.int32, (n_keys, BLOCK), 0)
    qi = lax.broadcasted_iota(jnp.int32, (n_keys, BLOCK), 1)
    dist = jnp.abs(BLOCK + qi - kc)
    neg_dist = -dist.astype(F32)
    inside = dist <= WINDOW
    for variant in range(N_BIAS_VARIANTS):
        ok = inside
        if variant == 1:
            ok = ok & (kc >= BLOCK)
        if variant == 2:
            ok = ok & (kc < 2 * BLOCK)
        dist_ref[variant] = jnp.where(ok, neg_dist, MASKED_DISTANCE).astype(BF16)
    row = lax.broadcasted_iota(jnp.int32, (BLOCK, LANES), 0)
    col = lax.broadcasted_iota(jnp.int32, (BLOCK, LANES), 1)
    for j in range(N_Q_HEADS):
        slope = 2.0 ** -(_slab_head(j) + 1)
        slope_ref[j * BLOCK:(j + 1) * BLOCK, :] = jnp.where(row == col, slope, 0.0).astype(BF16)


def _attention_block(n, n_blocks, q, sink_ref, k_ref, v_ref, dist_ref, slope_ref):
    low = lax.broadcasted_iota(jnp.int32, (BLOCK, LANES), 1) < HEAD_DIM
    zero = jnp.zeros((BLOCK, LANES), BF16)
    ones = jnp.ones((N_KEY_BLOCKS * BLOCK, LANES), BF16)
    prev_start = pl.multiple_of(jnp.maximum(n - 1, 0) * BLOCK, BLOCK)
    cur_start = pl.multiple_of(n * BLOCK, BLOCK)
    next_start = pl.multiple_of(jnp.minimum(n + 1, n_blocks - 1) * BLOCK, BLOCK)
    starts = (prev_start, cur_start, next_start)
    k3 = jnp.concatenate([k_ref[0, pl.ds(s, BLOCK), :] for s in starts], axis=0)
    v3 = jnp.concatenate([v_ref[0, pl.ds(s, BLOCK), :] for s in starts], axis=0)
    variant = jnp.where(n == 0, 1, jnp.where(n == n_blocks - 1, 2, 0))
    k_aug = jnp.concatenate([k3, dist_ref[variant]], axis=1)
    v_aug = jnp.concatenate([v3, ones], axis=1)

    slabs = []
    for g in range(D_ATTN // LANES):
        qg = q[:, g * LANES:(g + 1) * LANES]
        slabs.append(jnp.where(low, qg, zero))
        slabs.append(jnp.where(low, zero, qg))
    q_aug = jnp.concatenate([jnp.concatenate(slabs, axis=0), slope_ref[...]], axis=1)
    logits_all = lax.dot_general(q_aug, k_aug, (((1,), (1,)), ((), ())),
                                 preferred_element_type=F32)
    outs = []
    for j in range(N_Q_HEADS):
        sink = sink_ref[_slab_head(j)]
        logits = logits_all[j * BLOCK:(j + 1) * BLOCK]
        m = jnp.max(logits, axis=-1, keepdims=True)
        p = jnp.exp(logits - m).astype(BF16)
        pv = jnp.dot(p, v_aug, preferred_element_type=F32)
        denom = pv[:, LANES:] + jnp.exp(sink - m)
        outs.append(pv[:, :LANES] * (1.0 / denom))
    return [jnp.where(low, outs[2 * g], outs[2 * g + 1]) for g in range(D_ATTN // LANES)]


def _log_sigmoid(x):
    return jnp.minimum(x, 0.0) - jnp.log1p(jnp.exp(-jnp.abs(x)))


def _attn_lru_kernel(sink_ref, q_ref, k_ref, v_ref, xr_ref, wg_ref, p_ref, attn_ref, h_ref,
                     dist_ref, slope_ref, xpad_ref, af_ref, uf_ref, ab_ref, ub_ref, hf_ref, hb_ref):
    seq = xr_ref.shape[2]
    n_blocks = seq // BLOCK
    blocks_per_step = q_ref.shape[1] // BLOCK
    n_chunks = seq // LRU_CHUNK
    seg_len = seq // LRU_SEGMENTS
    seg_shift = seg_len.bit_length() - 1
    pitch = seg_len + LRU_SEGMENT_PAD
    piece = min(LRU_CHUNK, seg_len)
    halo = SUBLANES
    tile = pl.program_id(1)

    @pl.when((pl.program_id(0) == 0) & (tile == 0))
    def _init_tables():
        _init_attention_tables(dist_ref, slope_ref)

    def scan_row(t):
        return pl.multiple_of(t + (t >> seg_shift) * LRU_SEGMENT_PAD, LRU_SEGMENT_PAD)

    zeros = jnp.zeros((halo, LANES), F32)
    xpad_ref[0:halo, :] = zeros
    xpad_ref[seq + halo:seq + 2 * halo, :] = zeros

    def copy_body(c, carry):
        t0 = pl.multiple_of(c * LRU_CHUNK, LRU_CHUNK)
        xpad_ref[pl.ds(t0 + halo, LRU_CHUNK), :] = xr_ref[0, 0, pl.ds(t0, LRU_CHUNK), :]
        return carry

    lax.fori_loop(0, n_chunks, copy_body, 0)

    params = p_ref[0]
    conv_w = [params[P_CONV_W + j:P_CONV_W + j + 1] for j in range(CONV_WIDTH)]
    conv_b = params[P_CONV_B:P_CONV_B + 1]
    rates = [(0.5 * LRU_C * LOG2_E) * _log_sigmoid(params[P_LAM + d:P_LAM + d + 1]) for d in range(N_DIR)]
    a_refs = (af_ref, ab_ref)
    u_refs = (uf_ref, ub_ref)
    bias_cols = jnp.where(lax.broadcasted_iota(jnp.int32, (LRU_CHUNK, LANES), 1) < BIAS_SPLIT,
                          1.0, 0.0).astype(BF16)

    def recurrence_inputs(c):
        t0 = pl.multiple_of(c * LRU_CHUNK, LRU_CHUNK)
        x_half = conv_b
        for j in range(CONV_WIDTH):
            x_half = x_half + xpad_ref[pl.ds(t0 + halo - CONV_PAD_LEFT + j, LRU_CHUNK), :] * conv_w[j]
        x_aug = jnp.concatenate([x_half.astype(BF16), bias_cols], axis=1)
        t = jnp.tanh(jnp.dot(x_aug, wg_ref[0], preferred_element_type=F32))
        for d in range(N_DIR):
            t_r = t[:, 2 * LANES * d:2 * LANES * d + LANES]
            t_i = t[:, 2 * LANES * d + LANES:2 * LANES * (d + 1)]
            a = jnp.exp2(rates[d] + rates[d] * t_r)
            s = jnp.maximum(1.0 - a * a, 0.0)
            u = (s * lax.rsqrt(jnp.maximum(s, SQRT_FLOOR))) * (x_half + x_half * t_i)
            for lo in range(0, LRU_CHUNK, piece):
                row0 = scan_row(t0 + lo)
                a_refs[d][pl.ds(row0, piece), :] = a[lo:lo + piece]
                u_refs[d][pl.ds(row0, piece), :] = u[lo:lo + piece]

    def attention_block(i):
        r0 = pl.multiple_of(i * BLOCK, BLOCK)
        outs = _attention_block(tile * blocks_per_step + i, n_blocks, q_ref[0, pl.ds(r0, BLOCK), :],
                                sink_ref, k_ref, v_ref, dist_ref, slope_ref)
        for g, out in enumerate(outs):
            attn_ref[0, pl.ds(r0, BLOCK), g * LANES:(g + 1) * LANES] = out

    chunks_per_block = n_chunks // blocks_per_step

    def dense_body(i, carry):
        for j in range(chunks_per_block):
            recurrence_inputs(i * chunks_per_block + j)
        attention_block(i)
        return carry

    lax.fori_loop(0, blocks_per_step, dense_body, 0, unroll=8)

    def segment_rows(k, group):
        return pl.ds(group * SUBLANES * pitch + k, SUBLANES, stride=pitch)

    groups = range(LRU_SEGMENT_GROUPS)
    n_steps = seg_len // LRU_SCAN_UNROLL

    def summary_body(s, carry):
        prod_f, h_f, prod_b, h_b = [list(v) for v in carry]
        for i in range(LRU_SCAN_UNROLL):
            kf = s * LRU_SCAN_UNROLL + i
            kb = seg_len - 1 - kf
            for g in groups:
                a = af_ref[segment_rows(kf, g), :]
                h_f[g] = a * h_f[g] + uf_ref[segment_rows(kf, g), :]
                prod_f[g] = prod_f[g] * a
                a = ab_ref[segment_rows(kb, g), :]
                h_b[g] = a * h_b[g] + ub_ref[segment_rows(kb, g), :]
                prod_b[g] = prod_b[g] * a
        return tuple(prod_f), tuple(h_f), tuple(prod_b), tuple(h_b)

    ones = tuple(jnp.ones((SUBLANES, LANES), F32) for _ in groups)
    zeros_g = tuple(jnp.zeros((SUBLANES, LANES), F32) for _ in groups)
    prod_f, end_f, prod_b, end_b = lax.fori_loop(0, n_steps, summary_body, (ones, zeros_g, ones, zeros_g))
    prod_f, end_f, prod_b, end_b = [jnp.concatenate(v, axis=0) for v in (prod_f, end_f, prod_b, end_b)]

    entry = jnp.zeros((1, LANES), F32)
    entries_f = [entry]
    for j in range(1, LRU_SEGMENTS):
        entry = prod_f[j - 1:j] * entry + end_f[j - 1:j]
        entries_f.append(entry)
    entry = jnp.zeros((1, LANES), F32)
    entries_b = [entry]
    for j in range(LRU_SEGMENTS - 2, -1, -1):
        entry = prod_b[j + 1:j + 2] * entry + end_b[j + 1:j + 2]
        entries_b.append(entry)
    entries_b = entries_b[::-1]
    start_f = tuple(jnp.concatenate(entries_f[g * SUBLANES:(g + 1) * SUBLANES], axis=0) for g in groups)
    start_b = tuple(jnp.concatenate(entries_b[g * SUBLANES:(g + 1) * SUBLANES], axis=0) for g in groups)

    def scan_body(s, carry):
        h_f, h_b = [list(v) for v in carry]
        for i in range(LRU_SCAN_UNROLL):
            kf = s * LRU_SCAN_UNROLL + i
            kb = seg_len - 1 - kf
            for g in groups:
                h_f[g] = af_ref[segment_rows(kf, g), :] * h_f[g] + uf_ref[segment_rows(kf, g), :]
                hf_ref[segment_rows(kf, g), :] = h_f[g]
                h_b[g] = ab_ref[segment_rows(kb, g), :] * h_b[g] + ub_ref[segment_rows(kb, g), :]
                hb_ref[segment_rows(kb, g), :] = h_b[g]
        return tuple(h_f), tuple(h_b)

    lax.fori_loop(0, n_steps, scan_body, (start_f, start_b))

    def out_body(c, carry):
        t0 = pl.multiple_of(c * LRU_CHUNK, LRU_CHUNK)
        for lo in range(0, LRU_CHUNK, piece):
            row0 = scan_row(t0 + lo)
            h_ref[0, 0, pl.ds(t0 + lo, piece), :] = hf_ref[pl.ds(row0, piece), :] + hb_ref[pl.ds(row0, piece), :]
        return carry

    lax.fori_loop(0, n_chunks, out_body, 0)


def _attention_and_lru(q, k, v, sink, xr, w_gates, params):
    batch, seq, _ = q.shape
    rows_q = seq // LRU_TILES
    seg_len = seq // LRU_SEGMENTS
    assert seq // BLOCK >= 2 and rows_q % BLOCK == 0
    assert (seq // LRU_CHUNK) % (rows_q // BLOCK) == 0
    assert seg_len & (seg_len - 1) == 0 and seg_len % LRU_SCAN_UNROLL == 0
    assert LRU_CHUNK % seg_len == 0 or seg_len % LRU_CHUNK == 0
    scan_rows = LRU_SEGMENTS * (seg_len + LRU_SEGMENT_PAD)
    slab = pl.BlockSpec((1, 1, seq, LANES), lambda b, c: (b, c, 0, 0))
    return pl.pallas_call(
        _attn_lru_kernel,
        out_shape=(
            jax.ShapeDtypeStruct((batch, seq, D_ATTN), F32),
            jax.ShapeDtypeStruct((batch, LRU_TILES, seq, LANES), F32),
        ),
        grid=(batch, LRU_TILES),
        in_specs=[
            pl.BlockSpec(memory_space=pltpu.SMEM),
            pl.BlockSpec((1, rows_q, D_ATTN), lambda b, c: (b, c, 0)),
            pl.BlockSpec((1, seq, KV_DIM), lambda b, c: (b, 0, 0)),
            pl.BlockSpec((1, seq, KV_DIM), lambda b, c: (b, 0, 0)),
            slab,
            pl.BlockSpec((1, 2 * LANES, 2 * N_DIR * LANES), lambda b, c: (c, 0, 0)),
            pl.BlockSpec((1, P_ROWS, LANES), lambda b, c: (c, 0, 0)),
        ],
        out_specs=(
            pl.BlockSpec((1, rows_q, D_ATTN), lambda b, c: (b, c, 0)),
            slab,
        ),
        scratch_shapes=[
            pltpu.VMEM((N_BIAS_VARIANTS, N_KEY_BLOCKS * BLOCK, LANES), BF16),
            pltpu.VMEM((N_Q_HEADS * BLOCK, LANES), BF16),
            pltpu.VMEM((seq + 2 * SUBLANES, LANES), F32),
        ] + [pltpu.VMEM((scan_rows, LANES), F32) for _ in range(3 * N_DIR)],
        compiler_params=_compiler_params(2),
        name="attention_lru",
    )(sink, q, k, v, xr, w_gates, params)


def _gelu_tanh(x):
    cdf = 0.5 * (1.0 + jnp.tanh(float(np.sqrt(2.0 / np.pi)) * (x + 0.044715 * (x * x * x))))
    return x * cdf


def _mix_ffn_kernel(x_ref, attn_ref, lru_ref, gate_ref, mod_ref, ga_ref, gl_ref, wo_ref, g2_ref, wfi_ref,
                    wfo_ref, gf_ref, o_ref, act_ref, *, final):
    x = x_ref[0]
    mod = mod_ref[0]
    attn = _rms(attn_ref[0]) * ga_ref[...]
    lru = jnp.concatenate([lru_ref[0, c] * _gelu_tanh(gate_ref[0, c]) for c in range(LRU_TILES)], axis=-1)
    lru = _rms(lru) * gl_ref[...]
    mix = jnp.concatenate([attn, lru], axis=-1).astype(BF16)
    x = x + mod[2:3] * jnp.dot(mix, wo_ref[...], preferred_element_type=F32)

    h = (_rms(x) * (g2_ref[...] * (1.0 + mod[4:5])) + mod[3:4]).astype(BF16)
    for j in range(D_FF // FF_COL_TILE):
        lo = j * FF_COL_TILE
        g = jnp.dot(h, wfi_ref[:, lo:lo + FF_COL_TILE], preferred_element_type=F32)
        u = jnp.dot(h, wfi_ref[:, D_FF + lo:D_FF + lo + FF_COL_TILE], preferred_element_type=F32)
        act_ref[:, lo:lo + FF_COL_TILE] = ((g * jax.nn.sigmoid(g)) * u).astype(BF16)
    x = x + mod[5:6] * jnp.dot(act_ref[...], wfo_ref[...], preferred_element_type=F32)
    if final:
        x = _rms(x) * gf_ref[...]
    o_ref[0] = x


def _mix_ffn(x, attn, lru, gate, mod, g_attn, g_lru, w_out, g_norm2, w_ffn_in, w_ffn_out, g_final, layer):
    batch, seq, _ = x.shape
    tm = TOKEN_TILE
    final = layer == DEPTH - 1
    tok = lambda b, i: (b, i, 0)
    const2 = lambda b, i: (0, 0)
    of_layer = lambda b, i: (layer, 0, 0)
    lru_spec = pl.BlockSpec((1, LRU_TILES, tm, LANES), lambda b, i: (b, 0, i, 0))
    resident = pl.Buffered(1)
    return pl.pallas_call(
        functools.partial(_mix_ffn_kernel, final=final),
        out_shape=jax.ShapeDtypeStruct((batch, seq, D_MODEL), F32),
        grid=(batch, seq // tm),
        in_specs=[
            pl.BlockSpec((1, tm, D_MODEL), tok),
            pl.BlockSpec((1, tm, D_ATTN), tok),
            lru_spec,
            lru_spec,
            pl.BlockSpec((1, N_MOD, D_MODEL), lambda b, i: (b, 0, 0)),
            pl.BlockSpec((1, D_ATTN), const2),
            pl.BlockSpec((1, D_LRU), const2),
            pl.BlockSpec((None, D_ATTN + D_LRU, D_MODEL), of_layer, pipeline_mode=resident),
            pl.BlockSpec((1, D_MODEL), const2),
            pl.BlockSpec((None, D_MODEL, 2 * D_FF), of_layer, pipeline_mode=resident),
            pl.BlockSpec((None, D_FF, D_MODEL), of_layer, pipeline_mode=resident),
            pl.BlockSpec((1, D_MODEL), const2),
        ],
        out_specs=pl.BlockSpec((1, tm, D_MODEL), tok),
        scratch_shapes=[pltpu.VMEM((tm, D_FF), BF16)],
        compiler_params=_compiler_params(2),
        name="mix_ffn",
    )(x, attn, lru, gate, mod, g_attn, g_lru, w_out, g_norm2, w_ffn_in, w_ffn_out, g_final)


def _pair_heads(a, axis):
    shape = a.shape
    a = a.reshape(shape[:axis] + (N_KV_HEADS, Q_PER_KV, HEAD_DIM) + shape[axis + 1:])
    return jnp.swapaxes(a, axis, axis + 1).reshape(shape)


def _gate_weights(w_rg, b_rg, w_ig, b_ig):
    per_tile = LANES // LRU_BLOCK
    eye = jnp.eye(per_tile, dtype=F32)

    def dense(w):
        w = w.reshape(N_DIR, LRU_TILES, per_tile, LRU_BLOCK, LRU_BLOCK)
        full = w[:, :, :, :, None, :] * eye[None, None, :, None, :, None]
        return full.reshape(N_DIR, LRU_TILES, LANES, LANES)

    rg, ig = dense(w_rg), dense(w_ig)
    weights = jnp.concatenate([rg[0], ig[0], rg[1], ig[1]], axis=-1).astype(BF16)

    def tiles(b):
        return b.reshape(N_DIR, LRU_TILES, LANES)

    brg, big = tiles(b_rg), tiles(b_ig)
    rest = jnp.concatenate([brg[0], big[0], brg[1], big[1]], axis=-1) * 0.5
    terms = []
    for _ in range(BIAS_SPLIT):
        term = rest.astype(BF16)
        terms.append(term)
        rest = rest - term.astype(F32)
    bias_rows = jnp.stack(terms, axis=1)
    pad = jnp.zeros((LRU_TILES, LANES - BIAS_SPLIT, 2 * N_DIR * LANES), BF16)
    return jnp.concatenate([weights, bias_rows, pad], axis=1)


def _lru_params(conv_w, conv_b, lam):
    table = jnp.concatenate([0.5 * conv_w, 0.5 * conv_b[None], lam], axis=0)
    table = jnp.pad(table, ((0, P_ROWS - table.shape[0]), (0, 0)))
    return table.reshape(P_ROWS, LRU_TILES, LANES).transpose(1, 0, 2)


def kernel(x_prompt, x_sample, c_prompt, c_sample, w_mod, b_mod, g_norm1, w_in, sink, conv_w, conv_b,
           w_rg, b_rg, w_ig, b_ig, lam, g_attn_out, g_lru_out, w_out, g_norm2, w_ffn_in, w_ffn_out, g_final):
    n_prompt = c_prompt.shape[0]
    mod_all = _modulation(jnp.concatenate([c_prompt, c_sample], axis=0), w_mod, b_mod)
    mod_all = mod_all.reshape(DEPTH, -1, N_MOD, D_MODEL)

    w_in_b = w_in.astype(BF16)
    w_in_b = jnp.concatenate([_pair_heads(w_in_b[:, :, :D_ATTN], 2), w_in_b[:, :, D_ATTN:]], axis=-1)
    w_out_b = w_out.astype(BF16)
    w_out_b = jnp.concatenate([_pair_heads(w_out_b[:, :D_ATTN], 1), w_out_b[:, D_ATTN:]], axis=1)
    g_attn = _pair_heads(g_attn_out, 1)
    w_ffn_in_b = w_ffn_in.astype(BF16)
    w_ffn_out_b = w_ffn_out.astype(BF16)

    xs = [x_prompt, x_sample]
    for l in range(DEPTH):
        w_gates = _gate_weights(w_rg[l], b_rg[l], w_ig[l], b_ig[l])
        params = _lru_params(conv_w[l], conv_b[l], lam[l])
        mods = [mod_all[l, :n_prompt], mod_all[l, n_prompt:]]
        for t in range(2):
            x = xs[t]
            q, k, v, xr, gate = _in_projection(x, mods[t], g_norm1[l][None], w_in_b, l)
            attn, lru = _attention_and_lru(q, k, v, sink[l], xr, w_gates, params)
            xs[t] = _mix_ffn(x, attn, lru, gate, mods[t], g_attn[l][None], g_lru_out[l][None], w_out_b,
                             g_norm2[l][None], w_ffn_in_b, w_ffn_out_b, g_final[None], l)
    return (xs[0], xs[1])
```

```python
import functools

import jax
import jax.numpy as jnp
import numpy as np
from jax import lax
from jax.experimental import pallas as pl
from jax.experimental.pallas import tpu as pltpu

F32 = jnp.float32
BF16 = jnp.bfloat16

D_MODEL = 1024
DEPTH = 4
D_ATTN = 512
N_Q_HEADS = 8
N_KV_HEADS = 2
HEAD_DIM = 64
Q_PER_KV = N_Q_HEADS // N_KV_HEADS
KV_DIM = N_KV_HEADS * HEAD_DIM
WINDOW = 128
BLOCK = 128
D_LRU = 512
N_LRU_BLOCKS = 8
LRU_BLOCK = D_LRU // N_LRU_BLOCKS
CONV_WIDTH = 4
CONV_PAD_LEFT = 2
LRU_C = 8.0
N_DIR = 2
D_IN = D_ATTN + 2 * KV_DIM + 2 * D_LRU
D_FF = 2816
N_MOD = 6
EPS = 1e-6
LOG2_E = float(np.log2(np.e))

LANES = 128
SUBLANES = 8
VMEM_LIMIT_BYTES = 58 * 1024 * 1024

TOKEN_TILE = 1024
INPROJ_TILE = 1024
MOD_COL_TILE = 1536
FF_COL_TILE = 256
LRU_TILES = D_LRU // LANES
LRU_CHUNK = 256
LRU_SEGMENT_GROUPS = 4
LRU_SEGMENTS = SUBLANES * LRU_SEGMENT_GROUPS
LRU_SEGMENT_PAD = 4
LRU_SCAN_UNROLL = 16
SQRT_FLOOR = 1e-30
BIAS_SPLIT = 3
N_KEY_BLOCKS = 3
N_BIAS_VARIANTS = 3
MASKED_DISTANCE = -(2.0 ** 100)

P_CONV_W = 0
P_CONV_B = 4
P_LAM = 5
P_ROWS = 8


def _rms(x):
    return x * lax.rsqrt(jnp.mean(x * x, axis=-1, keepdims=True) + EPS)


def _compiler_params(n_grid_dims):
    return pltpu.CompilerParams(
        dimension_semantics=("arbitrary",) * n_grid_dims,
        vmem_limit_bytes=VMEM_LIMIT_BYTES,
    )


def _mod_kernel(c_ref, w_ref, b_ref, o_ref):
    c = c_ref[...]
    c_act = c * jax.nn.sigmoid(c)
    o_ref[0] = jnp.dot(c_act, w_ref[0], preferred_element_type=F32) + b_ref[0]


def _modulation(c_all, w_mod, b_mod):
    n_rows = c_all.shape[0]
    n_cols = N_MOD * D_MODEL
    return pl.pallas_call(
        _mod_kernel,
        out_shape=jax.ShapeDtypeStruct((DEPTH, n_rows, n_cols), F32),
        grid=(DEPTH, n_cols // MOD_COL_TILE),
        in_specs=[
            pl.BlockSpec((n_rows, D_MODEL), lambda l, j: (0, 0)),
            pl.BlockSpec((1, D_MODEL, MOD_COL_TILE), lambda l, j: (l, 0, j)),
            pl.BlockSpec((1, 1, MOD_COL_TILE), lambda l, j: (l, 0, j)),
        ],
        out_specs=pl.BlockSpec((1, n_rows, MOD_COL_TILE), lambda l, j: (l, 0, j)),
        compiler_params=_compiler_params(2),
        name="modulation",
    )(c_all, w_mod, b_mod.reshape(DEPTH, 1, n_cols))


def _inproj_kernel(x_ref, mod_ref, g_ref, w_ref, q_ref, k_ref, v_ref, xr_ref, gate_ref):
    x = x_ref[0]
    mod = mod_ref[0]
    h = _rms(x) * (g_ref[...] * (1.0 + mod[1:2])) + mod[0:1]
    z = jnp.dot(h.astype(BF16), w_ref[...], preferred_element_type=F32)
    q_ref[0] = (z[:, :D_ATTN] * (HEAD_DIM ** -0.5)).astype(BF16)
    k_ref[0] = z[:, D_ATTN:D_ATTN + KV_DIM].astype(BF16)
    v_ref[0] = z[:, D_ATTN + KV_DIM:D_ATTN + 2 * KV_DIM].astype(BF16)
    base = D_ATTN + 2 * KV_DIM
    for c in range(LRU_TILES):
        xr_ref[0, c] = z[:, base + c * LANES:base + (c + 1) * LANES]
        gate_ref[0, c] = z[:, base + D_LRU + c * LANES:base + D_LRU + (c + 1) * LANES]


def _in_projection(x, mod, g_norm1, w_in, layer):
    batch, seq, _ = x.shape
    tm = INPROJ_TILE
    tok = lambda b, i: (b, i, 0)
    lru_spec = pl.BlockSpec((1, LRU_TILES, tm, LANES), lambda b, i: (b, 0, i, 0))
    return pl.pallas_call(
        _inproj_kernel,
        out_shape=(
            jax.ShapeDtypeStruct((batch, seq, D_ATTN), BF16),
            jax.ShapeDtypeStruct((batch, seq, KV_DIM), BF16),
            jax.ShapeDtypeStruct((batch, seq, KV_DIM), BF16),
            jax.ShapeDtypeStruct((batch, LRU_TILES, seq, LANES), F32),
            jax.ShapeDtypeStruct((batch, LRU_TILES, seq, LANES), F32),
        ),
        grid=(batch, seq // tm),
        in_specs=[
            pl.BlockSpec((1, tm, D_MODEL), tok),
            pl.BlockSpec((1, N_MOD, D_MODEL), lambda b, i: (b, 0, 0)),
            pl.BlockSpec((1, D_MODEL), lambda b, i: (0, 0)),
            pl.BlockSpec((None, D_MODEL, D_IN), lambda b, i: (layer, 0, 0)),
        ],
        out_specs=(
            pl.BlockSpec((1, tm, D_ATTN), tok),
            pl.BlockSpec((1, tm, KV_DIM), tok),
            pl.BlockSpec((1, tm, KV_DIM), tok),
            lru_spec,
            lru_spec,
        ),
        compiler_params=_compiler_params(2),
        name="in_projection",
    )(x, mod, g_norm1, w_in)


def _slab_head(j):
    return j // 2 + Q_PER_KV * (j % 2)


def _init_attention_tables(dist_ref, slope_ref):
    n_keys = N_KEY_BLOCKS * BLOCK
    kc = lax.broadcasted_iota(jnp.int32, (n_keys, BLOCK), 0)
    qi = lax.broadcasted_iota(jnp.int32, (n_keys, BLOCK), 1)
    dist = jnp.abs(BLOCK + qi - kc)
    neg_dist = -dist.astype(F32)
    inside = dist <= WINDOW
    for variant in range(N_BIAS_VARIANTS):
        ok = inside
        if variant == 1:
            ok = ok & (kc >= BLOCK)
        if variant == 2:
            ok = ok & (kc < 2 * BLOCK)
        dist_ref[variant] = jnp.where(ok, neg_dist, MASKED_DISTANCE).astype(BF16)
    row = lax.broadcasted_iota(jnp.int32, (BLOCK, LANES), 0)
    col = lax.broadcasted_iota(jnp.int32, (BLOCK, LANES), 1)
    for j in range(N_Q_HEADS):
        slope = 2.0 ** -(_slab_head(j) + 1)
        slope_ref[j * BLOCK:(j + 1) * BLOCK, :] = jnp.where(row == col, slope, 0.0).astype(BF16)


def _attention_block(n, n_blocks, q, sink_ref, k_ref, v_ref, dist_ref, slope_ref):
    low = lax.broadcasted_iota(jnp.int32, (BLOCK, LANES), 1) < HEAD_DIM
    zero = jnp.zeros((BLOCK, LANES), BF16)
    ones = jnp.ones((N_KEY_BLOCKS * BLOCK, LANES), BF16)
    prev_start = pl.multiple_of(jnp.maximum(n - 1, 0) * BLOCK, BLOCK)
    cur_start = pl.multiple_of(n * BLOCK, BLOCK)
    next_start = pl.multiple_of(jnp.minimum(n + 1, n_blocks - 1) * BLOCK, BLOCK)
    starts = (prev_start, cur_start, next_start)
    k3 = jnp.concatenate([k_ref[0, pl.ds(s, BLOCK), :] for s in starts], axis=0)
    v3 = jnp.concatenate([v_ref[0, pl.ds(s, BLOCK), :] for s in starts], axis=0)
    variant = jnp.where(n == 0, 1, jnp.where(n == n_blocks - 1, 2, 0))
    k_aug = jnp.concatenate([k3, dist_ref[variant]], axis=1)
    v_aug = jnp.concatenate([v3, ones], axis=1)

    slabs = []
    for g in range(D_ATTN // LANES):
        qg = q[:, g * LANES:(g + 1) * LANES]
        slabs.append(jnp.where(low, qg, zero))
        slabs.append(jnp.where(low, zero, qg))
    q_aug = jnp.concatenate([jnp.concatenate(slabs, axis=0), slope_ref[...]], axis=1)
    logits_all = lax.dot_general(q_aug, k_aug, (((1,), (1,)), ((), ())),
                                 preferred_element_type=F32)
    outs = []
    for j in range(N_Q_HEADS):
        sink = sink_ref[_slab_head(j)]
        logits = logits_all[j * BLOCK:(j + 1) * BLOCK]
        m = jnp.max(logits, axis=-1, keepdims=True)
        p = jnp.exp(logits - m).astype(BF16)
        pv = jnp.dot(p, v_aug, preferred_element_type=F32)
        denom = pv[:, LANES:] + jnp.exp(sink - m)
        outs.append(pv[:, :LANES] * (1.0 / denom))
    return [jnp.where(low, outs[2 * g], outs[2 * g + 1]) for g in range(D_ATTN // LANES)]


def _log_sigmoid(x):
    return jnp.minimum(x, 0.0) - jnp.log1p(jnp.exp(-jnp.abs(x)))


def _attn_lru_kernel(sink_ref, q_ref, k_ref, v_ref, xr_ref, wg_ref, p_ref, attn_ref, h_ref,
                     dist_ref, slope_ref, xpad_ref, af_ref, uf_ref, ab_ref, ub_ref, hf_ref, hb_ref):
    seq = xr_ref.shape[2]
    n_blocks = seq // BLOCK
    blocks_per_step = q_ref.shape[1] // BLOCK
    n_chunks = seq // LRU_CHUNK
    seg_len = seq // LRU_SEGMENTS
    seg_shift = seg_len.bit_length() - 1
    pitch = seg_len + LRU_SEGMENT_PAD
    piece = min(LRU_CHUNK, seg_len)
    halo = SUBLANES
    tile = pl.program_id(1)

    @pl.when((pl.program_id(0) == 0) & (tile == 0))
    def _init_tables():
        _init_attention_tables(dist_ref, slope_ref)

    def scan_row(t):
        return pl.multiple_of(t + (t >> seg_shift) * LRU_SEGMENT_PAD, LRU_SEGMENT_PAD)

    zeros = jnp.zeros((halo, LANES), F32)
    xpad_ref[0:halo, :] = zeros
    xpad_ref[seq + halo:seq + 2 * halo, :] = zeros

    def copy_body(c, carry):
        t0 = pl.multiple_of(c * LRU_CHUNK, LRU_CHUNK)
        xpad_ref[pl.ds(t0 + halo, LRU_CHUNK), :] = xr_ref[0, 0, pl.ds(t0, LRU_CHUNK), :]
        return carry

    lax.fori_loop(0, n_chunks, copy_body, 0)

    params = p_ref[0]
    conv_w = [params[P_CONV_W + j:P_CONV_W + j + 1] for j in range(CONV_WIDTH)]
    conv_b = params[P_CONV_B:P_CONV_B + 1]
    rates = [(0.5 * LRU_C * LOG2_E) * _log_sigmoid(params[P_LAM + d:P_LAM + d + 1]) for d in range(N_DIR)]
    a_refs = (af_ref, ab_ref)
    u_refs = (uf_ref, ub_ref)
    bias_cols = jnp.where(lax.broadcasted_iota(jnp.int32, (LRU_CHUNK, LANES), 1) < BIAS_SPLIT,
                          1.0, 0.0).astype(BF16)

    def recurrence_inputs(c):
        t0 = pl.multiple_of(c * LRU_CHUNK, LRU_CHUNK)
        x_half = conv_b
        for j in range(CONV_WIDTH):
            x_half = x_half + xpad_ref[pl.ds(t0 + halo - CONV_PAD_LEFT + j, LRU_CHUNK), :] * conv_w[j]
        x_aug = jnp.concatenate([x_half.astype(BF16), bias_cols], axis=1)
        t = jnp.tanh(jnp.dot(x_aug, wg_ref[0], preferred_element_type=F32))
        for d in range(N_DIR):
            t_r = t[:, 2 * LANES * d:2 * LANES * d + LANES]
            t_i = t[:, 2 * LANES * d + LANES:2 * LANES * (d + 1)]
            a = jnp.exp2(rates[d] + rates[d] * t_r)
            s = jnp.maximum(1.0 - a * a, 0.0)
            u = (s * lax.rsqrt(jnp.maximum(s, SQRT_FLOOR))) * (x_half + x_half * t_i)
            for lo in range(0, LRU_CHUNK, piece):
                row0 = scan_row(t0 + lo)
                a_refs[d][pl.ds(row0, piece), :] = a[lo:lo + piece]
                u_refs[d][pl.ds(row0, piece), :] = u[lo:lo + piece]

    def attention_block(i):
        r0 = pl.multiple_of(i * BLOCK, BLOCK)
        outs = _attention_block(tile * blocks_per_step + i, n_blocks, q_ref[0, pl.ds(r0, BLOCK), :],
                                sink_ref, k_ref, v_ref, dist_ref, slope_ref)
        for g, out in enumerate(outs):
            attn_ref[0, pl.ds(r0, BLOCK), g * LANES:(g + 1) * LANES] = out

    chunks_per_block = n_chunks // blocks_per_step

    def dense_body(i, carry):
        for j in range(chunks_per_block):
            recurrence_inputs(i * chunks_per_block + j)
        attention_block(i)
        return carry

    lax.fori_loop(0, blocks_per_step, dense_body, 0, unroll=8)

    def segment_rows(k, group):
        return pl.ds(group * SUBLANES * pitch + k, SUBLANES, stride=pitch)

    groups = range(LRU_SEGMENT_GROUPS)
    n_steps = seg_len // LRU_SCAN_UNROLL

    def summary_body(s, carry):
        prod_f, h_f, prod_b, h_b = [list(v) for v in carry]
        for i in range(LRU_SCAN_UNROLL):
            kf = s * LRU_SCAN_UNROLL + i
            kb = seg_len - 1 - kf
            for g in groups:
                a = af_ref[segment_rows(kf, g), :]
                h_f[g] = a * h_f[g] + uf_ref[segment_rows(kf, g), :]
                prod_f[g] = prod_f[g] * a
                a = ab_ref[segment_rows(kb, g), :]
                h_b[g] = a * h_b[g] + ub_ref[segment_rows(kb, g), :]
                prod_b[g] = prod_b[g] * a
        return tuple(prod_f), tuple(h_f), tuple(prod_b), tuple(h_b)

    ones = tuple(jnp.ones((SUBLANES, LANES), F32) for _ in groups)
    zeros_g = tuple(jnp.zeros((SUBLANES, LANES), F32) for _ in groups)
    prod_f, end_f, prod_b, end_b = lax.fori_loop(0, n_steps, summary_body, (ones, zeros_g, ones, zeros_g))
    prod_f, end_f, prod_b, end_b = [jnp.concatenate(v, axis=0) for v in (prod_f, end_f, prod_b, end_b)]

    entry = jnp.zeros((1, LANES), F32)
    entries_f = [entry]
    for j in range(1, LRU_SEGMENTS):
        entry = prod_f[j - 1:j] * entry + end_f[j - 1:j]
        entries_f.append(entry)
    entry = jnp.zeros((1, LANES), F32)
    entries_b = [entry]
    for j in range(LRU_SEGMENTS - 2, -1, -1):
        entry = prod_b[j + 1:j + 2] * entry + end_b[j + 1:j + 2]
        entries_b.append(entry)
    entries_b = entries_b[::-1]
    start_f = tuple(jnp.concatenate(entries_f[g * SUBLANES:(g + 1) * SUBLANES], axis=0) for g in groups)
    start_b = tuple(jnp.concatenate(entries_b[g * SUBLANES:(g + 1) * SUBLANES], axis=0) for g in groups)

    def scan_body(s, carry):
        h_f, h_b = [list(v) for v in carry]
        for i in range(LRU_SCAN_UNROLL):
            kf = s * LRU_SCAN_UNROLL + i
            kb = seg_len - 1 - kf
            for g in groups:
                h_f[g] = af_ref[segment_rows(kf, g), :] * h_f[g] + uf_ref[segment_rows(kf, g), :]
                hf_ref[segment_rows(kf, g), :] = h_f[g]
                h_b[g] = ab_ref[segment_rows(kb, g), :] * h_b[g] + ub_ref[segment_rows(kb, g), :]
                hb_ref[segment_rows(kb, g), :] = h_b[g]
        return tuple(h_f), tuple(h_b)

    lax.fori_loop(0, n_steps, scan_body, (start_f, start_b))

    def out_body(c, carry):
        t0 = pl.multiple_of(c * LRU_CHUNK, LRU_CHUNK)
        for lo in range(0, LRU_CHUNK, piece):
            row0 = scan_row(t0 + lo)
            h_ref[0, 0, pl.ds(t0 + lo, piece), :] = hf_ref[pl.ds(row0, piece), :] + hb_ref[pl.ds(row0, piece), :]
        return carry

    lax.fori_loop(0, n_chunks, out_body, 0)


def _attention_and_lru(q, k, v, sink, xr, w_gates, params):
    batch, seq, _ = q.shape
    rows_q = seq // LRU_TILES
    seg_len = seq // LRU_SEGMENTS
    assert seq // BLOCK >= 2 and rows_q % BLOCK == 0
    assert (seq // LRU_CHUNK) % (rows_q // BLOCK) == 0
    assert seg_len & (seg_len - 1) == 0 and seg_len % LRU_SCAN_UNROLL == 0
    assert LRU_CHUNK % seg_len == 0 or seg_len % LRU_CHUNK == 0
    scan_rows = LRU_SEGMENTS * (seg_len + LRU_SEGMENT_PAD)
    slab = pl.BlockSpec((1, 1, seq, LANES), lambda b, c: (b, c, 0, 0))
    return pl.pallas_call(
        _attn_lru_kernel,
        out_shape=(
            jax.ShapeDtypeStruct((batch, seq, D_ATTN), F32),
            jax.ShapeDtypeStruct((batch, LRU_TILES, seq, LANES), F32),
        ),
        grid=(batch, LRU_TILES),
        in_specs=[
            pl.BlockSpec(memory_space=pltpu.SMEM),
            pl.BlockSpec((1, rows_q, D_ATTN), lambda b, c: (b, c, 0)),
            pl.BlockSpec((1, seq, KV_DIM), lambda b, c: (b, 0, 0)),
            pl.BlockSpec((1, seq, KV_DIM), lambda b, c: (b, 0, 0)),
            slab,
            pl.BlockSpec((1, 2 * LANES, 2 * N_DIR * LANES), lambda b, c: (c, 0, 0)),
            pl.BlockSpec((1, P_ROWS, LANES), lambda b, c: (c, 0, 0)),
        ],
        out_specs=(
            pl.BlockSpec((1, rows_q, D_ATTN), lambda b, c: (b, c, 0)),
            slab,
        ),
        scratch_shapes=[
            pltpu.VMEM((N_BIAS_VARIANTS, N_KEY_BLOCKS * BLOCK, LANES), BF16),
            pltpu.VMEM((N_Q_HEADS * BLOCK, LANES), BF16),
            pltpu.VMEM((seq + 2 * SUBLANES, LANES), F32),
        ] + [pltpu.VMEM((scan_rows, LANES), F32) for _ in range(3 * N_DIR)],
        compiler_params=_compiler_params(2),
        name="attention_lru",
    )(sink, q, k, v, xr, w_gates, params)


def _gelu_tanh(x):
    cdf = 0.5 * (1.0 + jnp.tanh(float(np.sqrt(2.0 / np.pi)) * (x + 0.044715 * (x * x * x))))
    return x * cdf


def _mix_ffn_kernel(x_ref, attn_ref, lru_ref, gate_ref, mod_ref, ga_ref, gl_ref, wo_ref, g2_ref, wfi_ref,
                    wfo_ref, gf_ref, o_ref, act_ref, *, final):
    x = x_ref[0]
    mod = mod_ref[0]
    attn = _rms(attn_ref[0]) * ga_ref[...]
    lru = jnp.concatenate([lru_ref[0, c] * _gelu_tanh(gate_ref[0, c]) for c in range(LRU_TILES)], axis=-1)
    lru = _rms(lru) * gl_ref[...]
    mix = jnp.concatenate([attn, lru], axis=-1).astype(BF16)
    x = x + mod[2:3] * jnp.dot(mix, wo_ref[...], preferred_element_type=F32)

    h = (_rms(x) * (g2_ref[...] * (1.0 + mod[4:5])) + mod[3:4]).astype(BF16)
    for j in range(D_FF // FF_COL_TILE):
        lo = j * FF_COL_TILE
        g = jnp.dot(h, wfi_ref[:, lo:lo + FF_COL_TILE], preferred_element_type=F32)
        u = jnp.dot(h, wfi_ref[:, D_FF + lo:D_FF + lo + FF_COL_TILE], preferred_element_type=F32)
        act_ref[:, lo:lo + FF_COL_TILE] = ((g * jax.nn.sigmoid(g)) * u).astype(BF16)
    x = x + mod[5:6] * jnp.dot(act_ref[...], wfo_ref[...], preferred_element_type=F32)
    if final:
        x = _rms(x) * gf_ref[...]
    o_ref[0] = x


def _mix_ffn(x, attn, lru, gate, mod, g_attn, g_lru, w_out, g_norm2, w_ffn_in, w_ffn_out, g_final, layer):
    batch, seq, _ = x.shape
    tm = TOKEN_TILE
    final = layer == DEPTH - 1
    tok = lambda b, i: (b, i, 0)
    const2 = lambda b, i: (0, 0)
    of_layer = lambda b, i: (layer, 0, 0)
    lru_spec = pl.BlockSpec((1, LRU_TILES, tm, LANES), lambda b, i: (b, 0, i, 0))
    resident = pl.Buffered(1)
    return pl.pallas_call(
        functools.partial(_mix_ffn_kernel, final=final),
        out_shape=jax.ShapeDtypeStruct((batch, seq, D_MODEL), F32),
        grid=(batch, seq // tm),
        in_specs=[
            pl.BlockSpec((1, tm, D_MODEL), tok),
            pl.BlockSpec((1, tm, D_ATTN), tok),
            lru_spec,
            lru_spec,
            pl.BlockSpec((1, N_MOD, D_MODEL), lambda b, i: (b, 0, 0)),
            pl.BlockSpec((1, D_ATTN), const2),
            pl.BlockSpec((1, D_LRU), const2),
            pl.BlockSpec((None, D_ATTN + D_LRU, D_MODEL), of_layer, pipeline_mode=resident),
            pl.BlockSpec((1, D_MODEL), const2),
            pl.BlockSpec((None, D_MODEL, 2 * D_FF), of_layer, pipeline_mode=resident),
            pl.BlockSpec((None, D_FF, D_MODEL), of_layer, pipeline_mode=resident),
            pl.BlockSpec((1, D_MODEL), const2),
        ],
        out_specs=pl.BlockSpec((1, tm, D_MODEL), tok),
        scratch_shapes=[pltpu.VMEM((tm, D_FF), BF16)],
        compiler_params=_compiler_params(2),
        name="mix_ffn",
    )(x, attn, lru, gate, mod, g_attn, g_lru, w_out, g_norm2, w_ffn_in, w_ffn_out, g_final)


def _pair_heads(a, axis):
    shape = a.shape
    a = a.reshape(shape[:axis] + (N_KV_HEADS, Q_PER_KV, HEAD_DIM) + shape[axis + 1:])
    return jnp.swapaxes(a, axis, axis + 1).reshape(shape)


def _gate_weights(w_rg, b_rg, w_ig, b_ig):
    per_tile = LANES // LRU_BLOCK
    eye = jnp.eye(per_tile, dtype=F32)

    def dense(w):
        w = w.reshape(N_DIR, LRU_TILES, per_tile, LRU_BLOCK, LRU_BLOCK)
        full = w[:, :, :, :, None, :] * eye[None, None, :, None, :, None]
        return full.reshape(N_DIR, LRU_TILES, LANES, LANES)

    rg, ig = dense(w_rg), dense(w_ig)
    weights = jnp.concatenate([rg[0], ig[0], rg[1], ig[1]], axis=-1).astype(BF16)

    def tiles(b):
        return b.reshape(N_DIR, LRU_TILES, LANES)

    brg, big = tiles(b_rg), tiles(b_ig)
    rest = jnp.concatenate([brg[0], big[0], brg[1], big[1]], axis=-1) * 0.5
    terms = []
    for _ in range(BIAS_SPLIT):
        term = rest.astype(BF16)
        terms.append(term)
        rest = rest - term.astype(F32)
    bias_rows = jnp.stack(terms, axis=1)
    pad = jnp.zeros((LRU_TILES, LANES - BIAS_SPLIT, 2 * N_DIR * LANES), BF16)
    return jnp.concatenate([weights, bias_rows, pad], axis=1)


def _lru_params(conv_w, conv_b, lam):
    table = jnp.concatenate([0.5 * conv_w, 0.5 * conv_b[None], lam], axis=0)
    table = jnp.pad(table, ((0, P_ROWS - table.shape[0]), (0, 0)))
    return table.reshape(P_ROWS, LRU_TILES, LANES).transpose(1, 0, 2)


def kernel(x_prompt, x_sample, c_prompt, c_sample, w_mod, b_mod, g_norm1, w_in, sink, conv_w, conv_b,
           w_rg, b_rg, w_ig, b_ig, lam, g_attn_out, g_lru_out, w_out, g_norm2, w_ffn_in, w_ffn_out, g_final):
    n_prompt = c_prompt.shape[0]
    mod_all = _modulation(jnp.concatenate([c_prompt, c_sample], axis=0), w_mod, b_mod)
    mod_all = mod_all.reshape(DEPTH, -1, N_MOD, D_MODEL)

    w_in_b = w_in.astype(BF16)
    w_in_b = jnp.concatenate([_pair_heads(w_in_b[:, :, :D_ATTN], 2), w_in_b[:, :, D_ATTN:]], axis=-1)
    w_out_b = w_out.astype(BF16)
    w_out_b = jnp.concatenate([_pair_heads(w_out_b[:, :D_ATTN], 1), w_out_b[:, D_ATTN:]], axis=1)
    g_attn = _pair_heads(g_attn_out, 1)
    w_ffn_in_b = w_ffn_in.astype(BF16)
    w_ffn_out_b = w_ffn_out.astype(BF16)

    xs = [x_prompt, x_sample]
    for l in range(DEPTH):
        w_gates = _gate_weights(w_rg[l], b_rg[l], w_ig[l], b_ig[l])
        params = _lru_params(conv_w[l], conv_b[l], lam[l])
        mods = [mod_all[l, :n_prompt], mod_all[l, n_prompt:]]
        for t in range(2):
            x = xs[t]
            q, k, v, xr, gate = _in_projection(x, mods[t], g_norm1[l][None], w_in_b, l)
            attn, lru = _attention_and_lru(q, k, v, sink[l], xr, w_gates, params)
            xs[t] = _mix_ffn(x, attn, lru, gate, mods[t], g_attn[l][None], g_lru_out[l][None], w_out_b,
                             g_norm2[l][None], w_ffn_in_b, w_ffn_out_b, g_final[None], l)
    return (xs[0], xs[1])
```
